```python
import jax, jax.numpy as jnp
from jax import lax
import numpy as np

D_MODEL = 1024
BATCH = 32
SEQ = 2048
DEPTH = 1
DEC_BATCH = 8
DEC_SEQ = 16
PAST_LEN = 2048

CHUNK = 64
HG_HEADS = 4
HG_DK = 128
HG_DV = 128
HG_WIDTH = HG_HEADS * HG_DV
ATTN_HEADS = 8
ATTN_KV_HEADS = 2
ATTN_HEAD_DIM = 64
ATTN_GROUP = ATTN_HEADS // ATTN_KV_HEADS
ATTN_WIDTH = ATTN_HEADS * ATTN_HEAD_DIM
MIX_WIDTH = HG_WIDTH + ATTN_WIDTH
WINDOW = 128
BAND = WINDOW // CHUNK + 1
MEM_TOKENS = 256
MEM_HEADS = 4
MEM_HEAD_DIM = D_MODEL // MEM_HEADS
D_FF = -(-8 * D_MODEL // (3 * 256)) * 256
IN_SPLITS = (HG_HEADS * HG_DK, HG_HEADS * HG_DK, HG_WIDTH, HG_WIDTH,
             ATTN_WIDTH, ATTN_KV_HEADS * ATTN_HEAD_DIM, ATTN_KV_HEADS * ATTN_HEAD_DIM)
IN_COLS = (2 * HG_HEADS * HG_DK + 2 * HG_WIDTH + ATTN_WIDTH
           + 2 * ATTN_KV_HEADS * ATTN_HEAD_DIM)
ALPHA = (2.0 * DEPTH) ** 0.25
BETA = (8.0 * DEPTH) ** -0.25
NEG = -1e30

kernel_name = "hymba_hgrn2_swa_sink_alibi_deepnorm_stream"


def layer_norm(x, g, b, eps=1e-5):
    xf = x.astype(jnp.float32)
    mu = xf.mean(-1, keepdims=True)
    var = jnp.square(xf - mu).mean(-1, keepdims=True)
    return ((xf - mu) * lax.rsqrt(var + eps) * g.astype(jnp.float32)
            + b.astype(jnp.float32)).astype(x.dtype)


def rms_norm(x, g, eps=1e-6):
    xf = x.astype(jnp.float32)
    return xf * lax.rsqrt(jnp.mean(jnp.square(xf), -1, keepdims=True) + eps) * g.astype(jnp.float32)


def split_in(z):
    return jnp.split(z, list(np.cumsum(IN_SPLITS)[:-1]), axis=-1)


def gla_chunkwise(q, k, v, logf, S0, L):
    B, T, H, dk = q.shape
    dv = v.shape[-1]
    N = T // L
    blk = lambda a: jnp.moveaxis(a.reshape(B, N, L, H, a.shape[-1]), 1, 0)
    qr, kr, vr = blk(q), blk(k), blk(v)
    Gr = jnp.cumsum(blk(logf), axis=2)
    mask = jnp.tril(jnp.ones((L, L), bool))

    def step(S, inp):
        qc, kc, vc, Gc = inp
        qg = qc * jnp.exp(Gc)
        kg = kc * jnp.exp(-Gc)
        A = jnp.where(mask, jnp.einsum('blhk,bshk->bhls', qg, kg), 0.0)
        o = jnp.einsum('bhls,bshv->blhv', A, vc) + jnp.einsum('blhk,bhkv->blhv', qg, S)
        GL = Gc[:, -1]
        kdec = kc * jnp.exp(GL[:, None] - Gc)
        S = jnp.exp(GL)[..., None] * S + jnp.einsum('bshk,bshv->bhkv', kdec, vc)
        return S, o

    S, o = lax.scan(step, S0, (qr, kr, vr, Gr))
    return jnp.moveaxis(o, 0, 1).reshape(B, T, H, dv), S


def hgrn_mixer(hq, hf, hi, hg, lb, g_norm, S0):
    B, T, _ = hq.shape
    hshape = (B, T, HG_HEADS, HG_DK)
    f = lb + (1.0 - lb) * jax.nn.sigmoid(hf.astype(jnp.float32))
    q = jax.nn.silu(hq.astype(jnp.float32)).reshape(hshape)
    k = (1.0 - f).reshape(hshape)
    logf = jnp.log(f).reshape(hshape)
    v = hi.astype(jnp.float32).reshape(B, T, HG_HEADS, HG_DV)
    o, S = gla_chunkwise(q, k, v, logf, S0.astype(jnp.float32), min(CHUNK, T))
    o = rms_norm(o, g_norm) * jax.nn.silu(hg.astype(jnp.float32)).reshape(B, T, HG_HEADS, HG_DV)
    return o.reshape(B, T, HG_WIDTH).astype(hq.dtype), S


def alibi_bias(qpos, kpos):
    slopes = jnp.asarray(2.0 ** (-8.0 * np.arange(1, ATTN_HEADS + 1) / ATTN_HEADS), jnp.float32)
    dist = jnp.abs(qpos[:, None] - kpos[None, :]).astype(jnp.float32)
    return -slopes.reshape(ATTN_KV_HEADS, ATTN_GROUP, 1, 1) * dist


def sink_attention(q, k, v, bias, valid, sinks):
    s = jnp.einsum('...qhgd,...khd->...hgqk', q, k).astype(jnp.float32) * ATTN_HEAD_DIM ** -0.5 + bias
    if valid is not None:
        s = jnp.where(valid, s, NEG)
    sink = jnp.broadcast_to(sinks.astype(jnp.float32).reshape(ATTN_KV_HEADS, ATTN_GROUP, 1, 1),
                            s.shape[:-1] + (1,))
    p = jax.nn.softmax(jnp.concatenate([s, sink], axis=-1), axis=-1)[..., :-1]
    return jnp.einsum('...hgqk,...khd->...qhgd', p.astype(v.dtype), v)


def band_blocks(a):
    B, T = a.shape[:2]
    N = T // CHUNK
    ac = a.reshape(B, N, CHUNK, ATTN_KV_HEADS, ATTN_HEAD_DIM)
    ap = jnp.concatenate([jnp.zeros((B, BAND - 1) + ac.shape[2:], a.dtype), ac], axis=1)
    blocks = jnp.stack([ap[:, i:i + N] for i in range(BAND)], axis=2)
    return blocks.reshape(B, N, BAND * CHUNK, ATTN_KV_HEADS, ATTN_HEAD_DIM)


def swa_prompt(aq, ak, av, sinks):
    B, T, _ = aq.shape
    N = T // CHUNK
    q = aq.reshape(B, N, CHUNK, ATTN_KV_HEADS, ATTN_GROUP, ATTN_HEAD_DIM)
    k_rows = ak.reshape(B, T, ATTN_KV_HEADS, ATTN_HEAD_DIM)
    v_rows = av.reshape(B, T, ATTN_KV_HEADS, ATTN_HEAD_DIM)
    kb, vb = band_blocks(k_rows), band_blocks(v_rows)
    bias = alibi_bias(jnp.arange(CHUNK) + (BAND - 1) * CHUNK, jnp.arange(BAND * CHUNK))
    key_chunk = jnp.arange(N)[:, None] - (BAND - 1) + jnp.arange(BAND * CHUNK)[None, :] // CHUNK
    valid = (key_chunk >= 0)[:, None, None, None, :]
    o = sink_attention(q, kb, vb, bias, valid, sinks)
    W = min(WINDOW, T)
    return o.reshape(B, T, ATTN_WIDTH), k_rows[:, T - W:], v_rows[:, T - W:]


def swa_sample(aq, ak, av, k_cache, v_cache, sinks):
    B, T, _ = aq.shape
    W = k_cache.shape[1]
    q = aq.reshape(B, T, ATTN_KV_HEADS, ATTN_GROUP, ATTN_HEAD_DIM)
    k_new = ak.reshape(B, T, ATTN_KV_HEADS, ATTN_HEAD_DIM)
    v_new = av.reshape(B, T, ATTN_KV_HEADS, ATTN_HEAD_DIM)
    k = jnp.concatenate([k_cache.astype(k_new.dtype), k_new], axis=1)
    v = jnp.concatenate([v_cache.astype(v_new.dtype), v_new], axis=1)
    bias = alibi_bias(W + jnp.arange(T), jnp.arange(W + T))
    o = sink_attention(q, k, v, bias, None, sinks)
    return o.reshape(B, T, ATTN_WIDTH), k_new, v_new


def mem_kv(mem, w_mem_kv):
    B, M, _ = mem.shape
    mk, mv = jnp.split(mem @ w_mem_kv, 2, axis=-1)
    return (mk.reshape(B, M, MEM_HEADS, MEM_HEAD_DIM), mv.reshape(B, M, MEM_HEADS, MEM_HEAD_DIM))


def mem_attention(x, mk, mv, w_q, w_o):
    B, T, _ = x.shape
    q = (x @ w_q).reshape(B, T, MEM_HEADS, MEM_HEAD_DIM)
    s = jnp.einsum('bthd,bmhd->bhtm', q, mk.astype(q.dtype)).astype(jnp.float32) * MEM_HEAD_DIM ** -0.5
    p = jax.nn.softmax(s, axis=-1)
    o = jnp.einsum('bhtm,bmhd->bthd', p.astype(x.dtype), mv.astype(x.dtype)).reshape(B, T, D_MODEL)
    return o @ w_o


def swiglu(x, w_in, w_out):
    g, u = jnp.split(x @ w_in, 2, axis=-1)
    return (jax.nn.silu(g) * u) @ w_out


def post_blocks(x, mix, mk, mv, w_mem_q, w_mem_o, w_ffn_in, w_ffn_out, ln_g, ln_b):
    x = layer_norm(ALPHA * x + mix, ln_g[0], ln_b[0])
    x = layer_norm(ALPHA * x + mem_attention(x, mk, mv, w_mem_q, w_mem_o), ln_g[1], ln_b[1])
    x = layer_norm(ALPHA * x + swiglu(x, w_ffn_in, w_ffn_out), ln_g[2], ln_b[2])
    return x


def setup_inputs(seed: int = 0) -> dict:
    key = jax.random.key(seed)
    ks = jax.random.split(key, 20)
    nrm = lambda k, shape, scale: jax.random.normal(k, shape, jnp.float32) * scale
    W = min(WINDOW, PAST_LEN)
    return {
        "x_prompt": nrm(ks[0], (BATCH, SEQ, D_MODEL), 1.0),
        "x_sample": nrm(ks[1], (DEC_BATCH, DEC_SEQ, D_MODEL), 1.0),
        "cache_swa_k": nrm(ks[2], (DEPTH, DEC_BATCH, W, ATTN_KV_HEADS, ATTN_HEAD_DIM), 1.0),
        "cache_swa_v": nrm(ks[3], (DEPTH, DEC_BATCH, W, ATTN_KV_HEADS, ATTN_HEAD_DIM), 1.0),
        "state_hgrn": nrm(ks[4], (DEPTH, DEC_BATCH, HG_HEADS, HG_DK, HG_DV), 0.5),
        "cache_mem_k": nrm(ks[5], (DEPTH, DEC_BATCH, MEM_TOKENS, MEM_HEADS, MEM_HEAD_DIM), 1.0),
        "cache_mem_v": nrm(ks[6], (DEPTH, DEC_BATCH, MEM_TOKENS, MEM_HEADS, MEM_HEAD_DIM), 1.0),
        "mem_prompt": nrm(ks[7], (BATCH, MEM_TOKENS, D_MODEL), 1.0),
        "w_in": nrm(ks[8], (DEPTH, D_MODEL, IN_COLS), D_MODEL ** -0.5),
        "hgrn_lb_logits": 1.0 + nrm(ks[9], (DEPTH + 1, HG_HEADS * HG_DK), 0.1),
        "hgrn_norm_g": 1.0 + nrm(ks[10], (DEPTH, HG_DV), 0.02),
        "attn_sinks": nrm(ks[11], (DEPTH, ATTN_HEADS), 0.5),
        "w_out": nrm(ks[12], (DEPTH, MIX_WIDTH, D_MODEL), MIX_WIDTH ** -0.5 * BETA),
        "w_mem_q": nrm(ks[13], (DEPTH, D_MODEL, D_MODEL), D_MODEL ** -0.5),
        "w_mem_kv": nrm(ks[14], (DEPTH, D_MODEL, 2 * D_MODEL), D_MODEL ** -0.5),
        "w_mem_o": nrm(ks[15], (DEPTH, D_MODEL, D_MODEL), D_MODEL ** -0.5 * BETA),
        "w_ffn_in": nrm(ks[16], (DEPTH, D_MODEL, 2 * D_FF), D_MODEL ** -0.5),
        "w_ffn_out": nrm(ks[17], (DEPTH, D_FF, D_MODEL), D_FF ** -0.5 * BETA),
        "ln_g": 1.0 + nrm(ks[18], (DEPTH, 3, D_MODEL), 0.02),
        "ln_b": nrm(ks[19], (DEPTH, 3, D_MODEL), 0.02),
    }


def reference(x_prompt, x_sample, cache_swa_k, cache_swa_v, state_hgrn, cache_mem_k, cache_mem_v,
              mem_prompt, w_in, hgrn_lb_logits, hgrn_norm_g, attn_sinks, w_out, w_mem_q, w_mem_kv,
              w_mem_o, w_ffn_in, w_ffn_out, ln_g, ln_b):
    lb_all = jnp.cumsum(jax.nn.softmax(hgrn_lb_logits.astype(jnp.float32), axis=0), axis=0)
    yp, ys = x_prompt, x_sample
    kp_l, vp_l, sp_l, mkp_l, mvp_l, ks_l, vs_l, ss_l = [], [], [], [], [], [], [], []
    for l in range(DEPTH):
        hq, hf, hi, hg, aq, ak, av = split_in(yp @ w_in[l])
        S0 = jnp.zeros((yp.shape[0], HG_HEADS, HG_DK, HG_DV), jnp.float32)
        o_h, S_p = hgrn_mixer(hq, hf, hi, hg, lb_all[l], hgrn_norm_g[l], S0)
        o_a, k_p, v_p = swa_prompt(aq, ak, av, attn_sinks[l])
        mix = jnp.concatenate([o_h, o_a], axis=-1) @ w_out[l]
        mk_p, mv_p = mem_kv(mem_prompt, w_mem_kv[l])
        yp = post_blocks(yp, mix, mk_p, mv_p, w_mem_q[l], w_mem_o[l], w_ffn_in[l], w_ffn_out[l],
                         ln_g[l], ln_b[l])
        hq, hf, hi, hg, aq, ak, av = split_in(ys @ w_in[l])
        o_h, S_s = hgrn_mixer(hq, hf, hi, hg, lb_all[l], hgrn_norm_g[l], state_hgrn[l])
        o_a, k_s, v_s = swa_sample(aq, ak, av, cache_swa_k[l], cache_swa_v[l], attn_sinks[l])
        mix = jnp.concatenate([o_h, o_a], axis=-1) @ w_out[l]
        ys = post_blocks(ys, mix, cache_mem_k[l], cache_mem_v[l], w_mem_q[l], w_mem_o[l],
                         w_ffn_in[l], w_ffn_out[l], ln_g[l], ln_b[l])
        kp_l.append(k_p); vp_l.append(v_p); sp_l.append(S_p)
        mkp_l.append(mk_p); mvp_l.append(mv_p)
        ks_l.append(k_s); vs_l.append(v_s); ss_l.append(S_s)
    new_swa_k_prompt = jnp.stack(kp_l)
    new_swa_v_prompt = jnp.stack(vp_l)
    new_hgrn_state_prompt = jnp.stack(sp_l)
    new_mem_k_prompt = jnp.stack(mkp_l)
    new_mem_v_prompt = jnp.stack(mvp_l)
    new_swa_k_sample = jnp.stack(ks_l)
    new_swa_v_sample = jnp.stack(vs_l)
    new_hgrn_state_sample = jnp.stack(ss_l)
    return (yp, ys, new_swa_k_prompt, new_swa_v_prompt, new_hgrn_state_prompt, new_mem_k_prompt,
            new_mem_v_prompt, new_swa_k_sample, new_swa_v_sample, new_hgrn_state_sample)
```

```python
import functools

import numpy as np
import jax
import jax.numpy as jnp
from jax import lax
from jax.experimental import pallas as pl
from jax.experimental.pallas import tpu as pltpu

F32 = jnp.float32
BF16 = jnp.bfloat16

CHUNK = 64
HG_HEADS = 4
HG_DK = 128
HG_DV = 128
HG_WIDTH = HG_HEADS * HG_DV
ATTN_HEADS = 8
ATTN_KV_HEADS = 2
ATTN_HEAD_DIM = 64
ATTN_GROUP = ATTN_HEADS // ATTN_KV_HEADS
ATTN_WIDTH = ATTN_HEADS * ATTN_HEAD_DIM
WINDOW = 128
BAND = WINDOW // CHUNK + 1
MEM_HEADS = 4
NEG = -1e30

C_HQ, C_HF, C_HI, C_HG = 0, 512, 1024, 1536
C_AQ = 2048
C_AK = C_AQ + ATTN_WIDTH
C_AV = C_AK + ATTN_KV_HEADS * ATTN_HEAD_DIM
IN_COLS = C_AV + ATTN_KV_HEADS * ATTN_HEAD_DIM

LANES = 128
MXU_COLS = 256
VMEM_LIMIT = 56 * 1024 * 1024


def _sigmoid(x):
    return 1.0 / (1.0 + jnp.exp(-x))


def _dot(a, b):
    return jnp.dot(a, b, preferred_element_type=F32)


def _dot_nt(a, b):
    return lax.dot_general(a, b, (((1,), (1,)), ((), ())), preferred_element_type=F32)


def _dot_tn(a, b):
    return lax.dot_general(a, b, (((0,), (0,)), ((), ())), preferred_element_type=F32)


def _layer_norm(y, g, b):
    mu = jnp.mean(y, axis=-1, keepdims=True)
    d = y - mu
    var = jnp.mean(d * d, axis=-1, keepdims=True)
    return d * lax.rsqrt(var + 1e-5) * g + b


def _lower_bound(lbl, layer):
    n = lbl.shape[0]
    rows = [lbl[i:i + 1, :] for i in range(n)]
    m = functools.reduce(jnp.maximum, rows)
    e = [jnp.exp(r - m) for r in rows]
    tot = functools.reduce(jnp.add, e)
    return functools.reduce(jnp.add, e[:layer + 1]) / tot


def _tril_consts(L):
    r = lax.broadcasted_iota(jnp.int32, (L, L), 0)
    c = lax.broadcasted_iota(jnp.int32, (L, L), 1)
    mask = c <= r
    tril = jnp.where(mask, 1.0, 0.0).astype(BF16)
    return mask, jnp.concatenate([tril, tril, tril], axis=1)


def _hgrn_chunk(hq, hf, hi, hg, lb, gnorm, st, tril_mask, tril3):
    L = hq.shape[0]
    f = lb + (1.0 - lb) * _sigmoid(hf)
    logf = jnp.log(f)
    p0 = logf.astype(BF16)
    r0 = logf - p0.astype(F32)
    p1 = r0.astype(BF16)
    p2 = (r0 - p1.astype(F32)).astype(BF16)
    G = _dot(tril3, jnp.concatenate([p0, p1, p2], axis=0))
    GL = G[L - 1:L, :]
    k = 1.0 - f
    qg = hq * _sigmoid(hq) * jnp.exp(G)
    kg = k * jnp.exp(-G)
    kdec = k * jnp.exp(GL - G)
    eGL = jnp.exp(GL)
    gate = hg * _sigmoid(hg)
    outs, new_st = [], []
    for h in range(HG_HEADS):
        sl = slice(h * HG_DK, (h + 1) * HG_DK)
        qg_h = qg[:, sl].astype(BF16)
        v_h = hi[:, sl].astype(BF16)
        A = jnp.where(tril_mask, _dot_nt(qg_h, kg[:, sl].astype(BF16)), 0.0)
        o = _dot(A.astype(BF16), v_h) + _dot_nt(qg_h, st[h].astype(BF16))
        new_st.append(st[h] * eGL[:, sl] + _dot_tn(v_h, kdec[:, sl].astype(BF16)))
        ms = jnp.mean(o * o, axis=-1, keepdims=True)
        outs.append(o * lax.rsqrt(ms + 1e-6) * gnorm * gate[:, sl])
    return jnp.concatenate(outs, axis=1), new_st


def _dup_halves(a, low):
    sw = pltpu.roll(a, ATTN_HEAD_DIM, axis=1)
    return jnp.where(low, a, sw), jnp.where(low, sw, a)


def _swa_group(qb0, qb1, k2, v2, bias, sink, low):
    Lq = qb0.shape[0]
    qm = jnp.concatenate([jnp.where(low, qb0, 0.0), jnp.where(low, 0.0, qb0),
                          jnp.where(low, qb1, 0.0), jnp.where(low, 0.0, qb1)], axis=0).astype(BF16)
    s = _dot_nt(qm, k2) * (ATTN_HEAD_DIM ** -0.5) + bias
    m = jnp.maximum(jnp.max(s, axis=-1, keepdims=True), sink)
    p = jnp.exp(s - m)
    den = jnp.sum(p, axis=-1, keepdims=True) + jnp.exp(sink - m)
    o = _dot(p.astype(BF16), v2) / den
    return (jnp.where(low, o[0:Lq], o[Lq:2 * Lq]), jnp.where(low, o[2 * Lq:3 * Lq], o[3 * Lq:4 * Lq]))


def _mixer_prompt_kernel(layer, tt, x_ref, w_in_ref, lbl_ref, gnorm_ref, bias_ref, sink_ref,
                         mix_ref, kp_ref, vp_ref, sp_ref, z_scr, k2_scr, v2_scr, st_scr):
    t = pl.program_id(1)
    nt = pl.num_programs(1)
    n_chunks = tt // CHUNK
    hist = (BAND - 1) * CHUNK

    z_scr[...] = _dot(x_ref[0].astype(BF16), w_in_ref[...])

    @pl.when(t == 0)
    def _():
        st_scr[...] = jnp.zeros_like(st_scr)
        k2_scr[:, 0:hist, :] = jnp.zeros((ATTN_KV_HEADS, hist, LANES), BF16)
        v2_scr[:, 0:hist, :] = jnp.zeros((ATTN_KV_HEADS, hist, LANES), BF16)

    @pl.when(t > 0)
    def _():
        k2_scr[:, 0:hist, :] = k2_scr[:, tt:tt + hist, :]
        v2_scr[:, 0:hist, :] = v2_scr[:, tt:tt + hist, :]

    low_t = lax.broadcasted_iota(jnp.int32, (tt, LANES), 1) < ATTN_HEAD_DIM
    ka, kb = _dup_halves(z_scr[:, C_AK:C_AK + LANES], low_t)
    k2_scr[0, hist:hist + tt, :] = ka.astype(BF16)
    k2_scr[1, hist:hist + tt, :] = kb.astype(BF16)
    va, vb = _dup_halves(z_scr[:, C_AV:C_AV + LANES], low_t)
    v2_scr[0, hist:hist + tt, :] = va.astype(BF16)
    v2_scr[1, hist:hist + tt, :] = vb.astype(BF16)

    lb = _lower_bound(lbl_ref[...], layer)
    gnorm = gnorm_ref[...]
    tril_mask, tril3 = _tril_consts(CHUNK)
    low = lax.broadcasted_iota(jnp.int32, (CHUNK, LANES), 1) < ATTN_HEAD_DIM

    def chunk_body(c, carry):
        r0 = pl.multiple_of(c * CHUNK, CHUNK)
        rows = pl.ds(r0, CHUNK)
        st = [st_scr[h] for h in range(HG_HEADS)]
        o_h, st = _hgrn_chunk(z_scr[rows, C_HQ:C_HQ + HG_WIDTH], z_scr[rows, C_HF:C_HF + HG_WIDTH],
                              z_scr[rows, C_HI:C_HI + HG_WIDTH], z_scr[rows, C_HG:C_HG + HG_WIDTH],
                              lb, gnorm, st, tril_mask, tril3)
        for h in range(HG_HEADS):
            st_scr[h] = st[h]
        mix_ref[0, rows, 0:HG_WIDTH] = o_h.astype(BF16)

        tb = jnp.minimum(t * n_chunks + c, BAND - 1)
        keys = pl.ds(r0, BAND * CHUNK)
        for g in range(ATTN_KV_HEADS):
            cq = C_AQ + g * ATTN_GROUP * ATTN_HEAD_DIM
            b0, b1 = _swa_group(z_scr[rows, cq:cq + LANES], z_scr[rows, cq + LANES:cq + 2 * LANES],
                                k2_scr[g, keys, :], v2_scr[g, keys, :], bias_ref[tb, g], sink_ref[g], low)
            co = HG_WIDTH + g * ATTN_GROUP * ATTN_HEAD_DIM
            mix_ref[0, rows, co:co + LANES] = b0.astype(BF16)
            mix_ref[0, rows, co + LANES:co + 2 * LANES] = b1.astype(BF16)
        return carry

    lax.fori_loop(0, n_chunks, chunk_body, 0)

    @pl.when(t == nt - 1)
    def _():
        w = min(WINDOW, tt)
        kp_ref[0] = z_scr[tt - w:tt, C_AK:C_AK + LANES]
        vp_ref[0] = z_scr[tt - w:tt, C_AV:C_AV + LANES]
        for h in range(HG_HEADS):
            sp_ref[0, h] = st_scr[h].T


def _alibi_tables(lq, lk, q0, with_chunk_mask):
    slopes = 2.0 ** (-8.0 * np.arange(1, ATTN_HEADS + 1) / ATTN_HEADS)
    dist = np.abs((q0 + np.arange(lq))[:, None] - np.arange(lk)[None, :]).astype(np.float64)
    base = -slopes.reshape(ATTN_KV_HEADS, ATTN_GROUP, 1, 1) * dist
    base = base.reshape(ATTN_KV_HEADS, ATTN_GROUP * lq, lk)
    if not with_chunk_mask:
        return jnp.asarray(base[None], F32)
    tabs = []
    for tb in range(BAND):
        valid = (np.arange(lk) // CHUNK) >= (BAND - 1 - tb)
        tabs.append(np.where(valid[None, None, :], base, NEG))
    return jnp.asarray(np.stack(tabs), F32)


def _sink_rows(sinks, lq):
    return jnp.repeat(sinks.astype(F32).reshape(ATTN_KV_HEADS, ATTN_GROUP), lq, axis=1)[..., None]


def _resident(shape):
    nd = len(shape)
    return pl.BlockSpec(shape, lambda *_: (0,) * nd, pipeline_mode=pl.Buffered(1))


def _mixers_prompt(x, w_in, lbl, gnorm, sinks, layer, tt):
    B, T, D = x.shape
    nt = T // tt
    hist = (BAND - 1) * CHUNK
    w = min(WINDOW, T)
    bias = _alibi_tables(CHUNK, BAND * CHUNK, hist, True)
    sink = _sink_rows(sinks, CHUNK)
    kern = functools.partial(_mixer_prompt_kernel, layer, tt)
    return pl.pallas_call(
        kern,
        grid=(B, nt),
        in_specs=[
            pl.BlockSpec((1, tt, D), lambda b, t: (b, t, 0)),
            _resident(w_in.shape),
            _resident(lbl.shape),
            _resident(gnorm.shape),
            _resident(bias.shape),
            _resident(sink.shape),
        ],
        out_specs=[
            pl.BlockSpec((1, tt, D), lambda b, t: (b, t, 0)),
            pl.BlockSpec((1, w, LANES), lambda b, t: (b, 0, 0)),
            pl.BlockSpec((1, w, LANES), lambda b, t: (b, 0, 0)),
            pl.BlockSpec((1, HG_HEADS, HG_DK, HG_DV), lambda b, t: (b, 0, 0, 0)),
        ],
        out_shape=[
            jax.ShapeDtypeStruct((B, T, D), BF16),
            jax.ShapeDtypeStruct((B, w, LANES), F32),
            jax.ShapeDtypeStruct((B, w, LANES), F32),
            jax.ShapeDtypeStruct((B, HG_HEADS, HG_DK, HG_DV), F32),
        ],
        scratch_shapes=[
            pltpu.VMEM((tt, IN_COLS), F32),
            pltpu.VMEM((ATTN_KV_HEADS, hist + tt, LANES), BF16),
            pltpu.VMEM((ATTN_KV_HEADS, hist + tt, LANES), BF16),
            pltpu.VMEM((HG_HEADS, HG_DV, HG_DK), F32),
        ],
        compiler_params=pltpu.CompilerParams(
            dimension_semantics=("arbitrary", "arbitrary"), vmem_limit_bytes=VMEM_LIMIT),
        name="mixers_prompt",
    )(x, w_in, lbl, gnorm, bias, sink)


def _mixer_sample_kernel(layer, nb, ts, x_ref, w_in_ref, lbl_ref, gnorm_ref, bias_ref, sink_ref,
                         kc_ref, vc_ref, s0_ref, mix_ref, kn_ref, vn_ref, sn_ref, z_scr):
    wlen = kc_ref.shape[1]
    z_scr[...] = _dot(x_ref[...].astype(BF16), w_in_ref[...])
    kn_ref[...] = z_scr[:, C_AK:C_AK + LANES]
    vn_ref[...] = z_scr[:, C_AV:C_AV + LANES]

    lb = _lower_bound(lbl_ref[...], layer)
    gnorm = gnorm_ref[...]
    tril_mask, tril3 = _tril_consts(ts)
    low = lax.broadcasted_iota(jnp.int32, (ts, LANES), 1) < ATTN_HEAD_DIM
    low_k = lax.broadcasted_iota(jnp.int32, (wlen + ts, LANES), 1) < ATTN_HEAD_DIM

    def batch_body(b, carry):
        r0 = pl.multiple_of(b * ts, ts)
        rows = pl.ds(r0, ts)
        st = [s0_ref[b, h].T for h in range(HG_HEADS)]
        o_h, st = _hgrn_chunk(z_scr[rows, C_HQ:C_HQ + HG_WIDTH], z_scr[rows, C_HF:C_HF + HG_WIDTH],
                              z_scr[rows, C_HI:C_HI + HG_WIDTH], z_scr[rows, C_HG:C_HG + HG_WIDTH],
                              lb, gnorm, st, tril_mask, tril3)
        for h in range(HG_HEADS):
            sn_ref[b, h] = st[h].T
        mix_ref[rows, 0:HG_WIDTH] = o_h.astype(BF16)

        k_all = jnp.concatenate([kc_ref[b], z_scr[rows, C_AK:C_AK + LANES]], axis=0)
        v_all = jnp.concatenate([vc_ref[b], z_scr[rows, C_AV:C_AV + LANES]], axis=0)
        k2 = _dup_halves(k_all, low_k)
        v2 = _dup_halves(v_all, low_k)
        for g in range(ATTN_KV_HEADS):
            cq = C_AQ + g * ATTN_GROUP * ATTN_HEAD_DIM
            b0, b1 = _swa_group(z_scr[rows, cq:cq + LANES], z_scr[rows, cq + LANES:cq + 2 * LANES],
                                k2[g].astype(BF16), v2[g].astype(BF16), bias_ref[0, g], sink_ref[g], low)
            co = HG_WIDTH + g * ATTN_GROUP * ATTN_HEAD_DIM
            mix_ref[rows, co:co + LANES] = b0.astype(BF16)
            mix_ref[rows, co + LANES:co + 2 * LANES] = b1.astype(BF16)
        return carry

    lax.fori_loop(0, nb, batch_body, 0)


def _mixers_sample(x, w_in, lbl, gnorm, sinks, k_cache, v_cache, s0, layer):
    nb, ts, D = x.shape
    wlen = k_cache.shape[1]
    bias = _alibi_tables(ts, wlen + ts, wlen, False)
    sink = _sink_rows(sinks, ts)
    kern = functools.partial(_mixer_sample_kernel, layer, nb, ts)
    n = nb * ts
    return pl.pallas_call(
        kern,
        out_shape=[
            jax.ShapeDtypeStruct((n, D), BF16),
            jax.ShapeDtypeStruct((n, LANES), F32),
            jax.ShapeDtypeStruct((n, LANES), F32),
            jax.ShapeDtypeStruct((nb, HG_HEADS, HG_DK, HG_DV), F32),
        ],
        scratch_shapes=[pltpu.VMEM((n, IN_COLS), F32)],
        compiler_params=pltpu.CompilerParams(vmem_limit_bytes=VMEM_LIMIT),
        name="mixers_sample",
    )(x.reshape(n, D), w_in, lbl, gnorm, bias, sink,
      k_cache.reshape(nb, wlen, LANES), v_cache.reshape(nb, wlen, LANES), s0)


def _mem_kv_kernel(d, m_ref, w_ref, k_ref, v_ref, kb_ref, vb_ref):
    r = _dot(m_ref[...].astype(BF16), w_ref[...])
    k_ref[...] = r[:, :d]
    v_ref[...] = r[:, d:]
    kb_ref[...] = r[:, :d].astype(BF16)
    vb_ref[...] = r[:, d:].astype(BF16)


def _mem_kv(mem, w_kv, tm):
    n, d = mem.shape
    row = lambda i: (i, 0)
    return pl.pallas_call(
        functools.partial(_mem_kv_kernel, d),
        grid=(n // tm,),
        in_specs=[pl.BlockSpec((tm, d), row), _resident(w_kv.shape)],
        out_specs=[pl.BlockSpec((tm, d), row)] * 4,
        out_shape=[jax.ShapeDtypeStruct((n, d), F32)] * 2 + [jax.ShapeDtypeStruct((n, d), BF16)] * 2,
        compiler_params=pltpu.CompilerParams(
            dimension_semantics=("arbitrary",), vmem_limit_bytes=VMEM_LIMIT),
        name="mem_kv",
    )(mem, w_kv)


def _post_kernel(alpha, nb, tb, x_ref, mix_ref, mk_ref, mv_ref, w_out_ref, w_q_ref, w_o_ref,
                 w_fi_ref, w_fo_ref, g_ref, b_ref, y_ref, o_scr, act_scr):
    d = x_ref.shape[-1]
    n = nb * tb
    dff = w_fo_ref.shape[0]
    hd = d // MEM_HEADS
    x = x_ref[...].reshape(n, d)
    mix = mix_ref[...].reshape(n, d)

    x1 = _layer_norm(alpha * x + _dot(mix, w_out_ref[...]), g_ref[0:1, :], b_ref[0:1, :])

    q = _dot(x1.astype(BF16), w_q_ref[...]).astype(BF16)
    for i in range(nb):
        rows = slice(i * tb, (i + 1) * tb)
        for h in range(MEM_HEADS):
            cols = slice(h * hd, (h + 1) * hd)
            s = _dot_nt(q[rows, cols], mk_ref[i, :, cols].astype(BF16)) * (hd ** -0.5)
            m = jnp.max(s, axis=-1, keepdims=True)
            p = jnp.exp(s - m)
            den = jnp.sum(p, axis=-1, keepdims=True)
            o = _dot(p.astype(BF16), mv_ref[i, :, cols].astype(BF16)) / den
            o_scr[rows, cols] = o.astype(BF16)
    x2 = _layer_norm(alpha * x1 + _dot(o_scr[...], w_o_ref[...]), g_ref[1:2, :], b_ref[1:2, :])

    x2b = x2.astype(BF16)
    for j in range(dff // MXU_COLS):
        cg = slice(j * MXU_COLS, (j + 1) * MXU_COLS)
        cu = slice(dff + j * MXU_COLS, dff + (j + 1) * MXU_COLS)
        gte = _dot(x2b, w_fi_ref[:, cg])
        up = _dot(x2b, w_fi_ref[:, cu])
        act_scr[:, cg] = (gte * _sigmoid(gte) * up).astype(BF16)
    x3 = _layer_norm(alpha * x2 + _dot(act_scr[...], w_fo_ref[...]), g_ref[2:3, :], b_ref[2:3, :])
    y_ref[...] = x3.reshape(nb, tb, d)


def _post_blocks(x, mix, mk, mv, w_out, w_q, w_o, w_fi, w_fo, ln_g, ln_b, alpha, nb, tb):
    B, T, d = x.shape
    dff = w_fo.shape[0]
    m_tok = mk.shape[1]
    n = nb * tb
    tile = lambda b, t: (b, t, 0)
    per_b = lambda b, t: (b, 0, 0)
    kern = functools.partial(_post_kernel, alpha, nb, tb)
    return pl.pallas_call(
        kern,
        grid=(B // nb, T // tb),
        in_specs=[
            pl.BlockSpec((nb, tb, d), tile),
            pl.BlockSpec((nb, tb, d), tile),
            pl.BlockSpec((nb, m_tok, d), per_b),
            pl.BlockSpec((nb, m_tok, d), per_b),
            _resident(w_out.shape), _resident(w_q.shape), _resident(w_o.shape),
            _resident(w_fi.shape), _resident(w_fo.shape),
            _resident(ln_g.shape), _resident(ln_b.shape),
        ],
        out_specs=pl.BlockSpec((nb, tb, d), tile),
        out_shape=jax.ShapeDtypeStruct((B, T, d), F32),
        scratch_shapes=[pltpu.VMEM((n, d), BF16), pltpu.VMEM((n, dff), BF16)],
        compiler_params=pltpu.CompilerParams(
            dimension_semantics=("arbitrary", "arbitrary"), vmem_limit_bytes=VMEM_LIMIT),
        name="post_blocks",
    )(x, mix, mk, mv, w_out, w_q, w_o, w_fi, w_fo, ln_g, ln_b)


TT_PROMPT = 512
TM_POST = 512
TM_MEM = 512


def kernel(x_prompt, x_sample, cache_swa_k, cache_swa_v, state_hgrn, cache_mem_k, cache_mem_v, mem_prompt, w_in, hgrn_lb_logits, hgrn_norm_g, attn_sinks, w_out, w_mem_q, w_mem_kv, w_mem_o, w_ffn_in, w_ffn_out, ln_g, ln_b):
    depth = w_in.shape[0]
    alpha = (2.0 * depth) ** 0.25
    B, T, D = x_prompt.shape
    nbs, ts, _ = x_sample.shape
    m_tok = mem_prompt.shape[1]
    lbl = hgrn_lb_logits.astype(F32)

    yp, ys = x_prompt, x_sample
    outs = [[] for _ in range(8)]
    for l in range(depth):
        w_in_l = w_in[l].astype(BF16)
        w_out_l = w_out[l].astype(BF16)
        w_q_l = w_mem_q[l].astype(BF16)
        w_kv_l = w_mem_kv[l].astype(BF16)
        w_o_l = w_mem_o[l].astype(BF16)
        w_fi_l = w_ffn_in[l].astype(BF16)
        w_fo_l = w_ffn_out[l].astype(BF16)
        gnorm = hgrn_norm_g[l].reshape(1, HG_DV).astype(F32)

        mix_p, k_p, v_p, s_p = _mixers_prompt(yp, w_in_l, lbl, gnorm, attn_sinks[l], l, TT_PROMPT)
        mk_p, mv_p, mk_b, mv_b = _mem_kv(mem_prompt.reshape(B * m_tok, D), w_kv_l, TM_MEM)
        yp = _post_blocks(yp, mix_p, mk_b.reshape(B, m_tok, D), mv_b.reshape(B, m_tok, D),
                          w_out_l, w_q_l, w_o_l, w_fi_l, w_fo_l, ln_g[l], ln_b[l], alpha, 1, TM_POST)

        mix_s, k_s, v_s, s_s = _mixers_sample(ys, w_in_l, lbl, gnorm, attn_sinks[l],
                                              cache_swa_k[l], cache_swa_v[l], state_hgrn[l], l)
        ys = _post_blocks(ys, mix_s.reshape(nbs, ts, D), cache_mem_k[l].reshape(nbs, m_tok, D),
                          cache_mem_v[l].reshape(nbs, m_tok, D),
                          w_out_l, w_q_l, w_o_l, w_fi_l, w_fo_l, ln_g[l], ln_b[l], alpha, nbs, ts)

        wlen = k_p.shape[1]
        new = [k_p.reshape(B, wlen, ATTN_KV_HEADS, ATTN_HEAD_DIM),
               v_p.reshape(B, wlen, ATTN_KV_HEADS, ATTN_HEAD_DIM),
               s_p,
               mk_p.reshape(B, m_tok, MEM_HEADS, D // MEM_HEADS),
               mv_p.reshape(B, m_tok, MEM_HEADS, D // MEM_HEADS),
               k_s.reshape(nbs, ts, ATTN_KV_HEADS, ATTN_HEAD_DIM),
               v_s.reshape(nbs, ts, ATTN_KV_HEADS, ATTN_HEAD_DIM),
               s_s]
        for acc, a in zip(outs, new):
            acc.append(a)
    return (yp, ys) + tuple(jnp.stack(a) for a in outs)
```

```python
import functools

import numpy as np
import jax
import jax.numpy as jnp
from jax import lax
from jax.experimental import pallas as pl
from jax.experimental.pallas import tpu as pltpu

F32 = jnp.float32
BF16 = jnp.bfloat16

CHUNK = 64
HG_HEADS = 4
HG_DK = 128
HG_DV = 128
HG_WIDTH = HG_HEADS * HG_DV
ATTN_HEADS = 8
ATTN_KV_HEADS = 2
ATTN_HEAD_DIM = 64
ATTN_GROUP = ATTN_HEADS // ATTN_KV_HEADS
ATTN_WIDTH = ATTN_HEADS * ATTN_HEAD_DIM
WINDOW = 128
BAND = WINDOW // CHUNK + 1
MEM_HEADS = 4
NEG = -1e30

C_HQ, C_HF, C_HI, C_HG = 0, 512, 1024, 1536
C_AQ = 2048
C_AK = C_AQ + ATTN_WIDTH
C_AV = C_AK + ATTN_KV_HEADS * ATTN_HEAD_DIM
IN_COLS = C_AV + ATTN_KV_HEADS * ATTN_HEAD_DIM

LANES = 128
MXU_COLS = 256
VMEM_LIMIT = 56 * 1024 * 1024


def _sigmoid(x):
    return 1.0 / (1.0 + jnp.exp(-x))


def _dot(a, b):
    return jnp.dot(a, b, preferred_element_type=F32)


def _dot_nt(a, b):
    return lax.dot_general(a, b, (((1,), (1,)), ((), ())), preferred_element_type=F32)


def _dot_tn(a, b):
    return lax.dot_general(a, b, (((0,), (0,)), ((), ())), preferred_element_type=F32)


def _layer_norm(y, g, b):
    mu = jnp.mean(y, axis=-1, keepdims=True)
    d = y - mu
    var = jnp.mean(d * d, axis=-1, keepdims=True)
    return d * lax.rsqrt(var + 1e-5) * g + b


def _lower_bound(lbl, layer):
    n = lbl.shape[0]
    rows = [lbl[i:i + 1, :] for i in range(n)]
    m = functools.reduce(jnp.maximum, rows)
    e = [jnp.exp(r - m) for r in rows]
    tot = functools.reduce(jnp.add, e)
    return functools.reduce(jnp.add, e[:layer + 1]) / tot


def _tril_consts(L):
    r = lax.broadcasted_iota(jnp.int32, (L, L), 0)
    c = lax.broadcasted_iota(jnp.int32, (L, L), 1)
    mask = c <= r
    tril = jnp.where(mask, 1.0, 0.0).astype(BF16)
    return mask, jnp.concatenate([tril, tril, tril], axis=1)


def _hgrn_gates(hq, hf, hi, lb, tril3):
    L = hq.shape[0]
    f = lb + (1.0 - lb) * _sigmoid(hf)
    logf = jnp.log(f)
    p0 = logf.astype(BF16)
    r0 = logf - p0.astype(F32)
    p1 = r0.astype(BF16)
    p2 = (r0 - p1.astype(F32)).astype(BF16)
    G = _dot(tril3, jnp.concatenate([p0, p1, p2], axis=0))
    eGL = jnp.exp(G[L - 1:L, :])
    qg = hq * _sigmoid(hq) * jnp.exp(G)
    kg = (1.0 - f) * jnp.exp(-G)
    return qg.astype(BF16), kg.astype(BF16), (kg * eGL).astype(BF16), hi.astype(BF16), eGL


def _head(h):
    return slice(h * HG_DK, (h + 1) * HG_DK)


def _hgrn_intra(qg, kg, kd, v, tril_mask):
    A = [jnp.where(tril_mask, _dot_nt(qg[:, _head(h)], kg[:, _head(h)]), 0.0).astype(BF16)
         for h in range(HG_HEADS)]
    U = [_dot_tn(v[:, _head(h)], kd[:, _head(h)]) for h in range(HG_HEADS)]
    return A, U


def _hgrn_out(A, qg, v, st_bf, hg, gnorm):
    gate = hg * _sigmoid(hg)
    outs = []
    for h in range(HG_HEADS):
        o = _dot(A[h], v[:, _head(h)]) + _dot_nt(qg[:, _head(h)], st_bf[h])
        ms = jnp.mean(o * o, axis=-1, keepdims=True)
        outs.append(o * lax.rsqrt(ms + 1e-6) * gnorm * gate[:, _head(h)])
    return jnp.concatenate(outs, axis=1)


def _dup_halves(a, low):
    sw = pltpu.roll(a, ATTN_HEAD_DIM, axis=1)
    return jnp.where(low, a, sw), jnp.where(low, sw, a)


def _swa_scores(qb0, qb1, k2, bias, low):
    return _dot_nt(_stack_q(qb0, qb1, low), k2) * (ATTN_HEAD_DIM ** -0.5) + bias


def _swa_probs(s, sink):
    m = jnp.maximum(jnp.max(s, axis=-1, keepdims=True), sink)
    p = jnp.exp(s - m)
    return p.astype(BF16), jnp.sum(p, axis=-1, keepdims=True) + jnp.exp(sink - m)


def _swa_out(p, den, v2, low):
    Lq = p.shape[0] // ATTN_GROUP
    o = _dot(p, v2) / den
    return (jnp.where(low, o[0:Lq], o[Lq:2 * Lq]), jnp.where(low, o[2 * Lq:3 * Lq], o[3 * Lq:4 * Lq]))


def _stack_q(qb0, qb1, low):
    return jnp.concatenate([jnp.where(low, qb0, 0.0), jnp.where(low, 0.0, qb0),
                            jnp.where(low, qb1, 0.0), jnp.where(low, 0.0, qb1)], axis=0).astype(BF16)


def _swa_probs_t(k2, qm, bias_t, sink_t):
    s = _dot_nt(k2, qm) * (ATTN_HEAD_DIM ** -0.5) + bias_t
    m = jnp.maximum(jnp.max(s, axis=0, keepdims=True), sink_t)
    e = jnp.exp(s - m)
    den = jnp.sum(e, axis=0, keepdims=True) + jnp.exp(sink_t - m)
    return (e * (1.0 / den)).astype(BF16)


def _swa_out_t(vt2, p_t, low):
    Lq = p_t.shape[1] // ATTN_GROUP
    o_t = _dot(vt2, p_t)
    a0 = o_t[:, 0:2 * Lq].T
    a1 = o_t[:, 2 * Lq:4 * Lq].T
    return jnp.where(low, a0[0:Lq], a0[Lq:2 * Lq]), jnp.where(low, a1[0:Lq], a1[Lq:2 * Lq])


def _mixer_prompt_kernel(layer, tt, x_ref, w_in_ref, lbl_ref, gnorm_ref, bias_ref, sink_ref,
                         mix_ref, kp_ref, vp_ref, sp_ref, z_scr, kh_scr, vh_scr, k2_scr, vt_scr, st_scr):
    t = pl.program_id(1)
    nt = pl.num_programs(1)
    n_chunks = tt // CHUNK
    hist = (BAND - 1) * CHUNK

    @pl.when(t == 0)
    def _():
        st_scr[...] = jnp.zeros_like(st_scr)
        kh_scr[...] = jnp.zeros_like(kh_scr)
        vh_scr[...] = jnp.zeros_like(vh_scr)

    z_scr[...] = _dot(x_ref[0].astype(BF16), w_in_ref[...])

    k_tile = z_scr[:, C_AK:C_AK + LANES]
    low_k = lax.broadcasted_iota(jnp.int32, (hist + tt, LANES), 1) < ATTN_HEAD_DIM
    ka, kb = _dup_halves(jnp.concatenate([kh_scr[...], k_tile], axis=0), low_k)
    k2_scr[0] = ka.astype(BF16)
    k2_scr[1] = kb.astype(BF16)
    v_tile = z_scr[:, C_AV:C_AV + LANES]
    v_hist = vh_scr[...]
    v_shift = (jnp.concatenate([v_hist, v_tile], axis=0),
               jnp.concatenate([v_hist[CHUNK:], v_tile, v_tile[tt - CHUNK:]], axis=0))
    for par in range(2):
        vt = v_shift[par].T.astype(BF16)
        for g in range(ATTN_KV_HEADS):
            blk = vt[g * ATTN_HEAD_DIM:(g + 1) * ATTN_HEAD_DIM]
            vt_scr[par, g] = jnp.concatenate([blk, blk], axis=0)
    kh_scr[...] = k_tile[tt - hist:]
    vh_scr[...] = v_tile[tt - hist:]

    lb = _lower_bound(lbl_ref[...], layer)
    gnorm = gnorm_ref[...]
    tril_mask, tril3 = _tril_consts(CHUNK)
    low = lax.broadcasted_iota(jnp.int32, (CHUNK, LANES), 1) < ATTN_HEAD_DIM
    chunks = range(n_chunks)
    groups = range(ATTN_KV_HEADS)
    rows = [slice(c * CHUNK, (c + 1) * CHUNK) for c in chunks]
    keys = [slice(c * CHUNK, (c + BAND) * CHUNK) for c in chunks]
    z = lambda c, col: z_scr[rows[c], col:col + HG_WIDTH]

    gates = [_hgrn_gates(z(c, C_HQ), z(c, C_HF), z(c, C_HI), lb, tril3) for c in chunks]

    def probs(c, g):
        tb = BAND - 1 if c >= BAND - 1 else jnp.minimum(t * n_chunks + c, BAND - 1)
        cq = C_AQ + g * ATTN_GROUP * ATTN_HEAD_DIM
        qm = _stack_q(z_scr[rows[c], cq:cq + LANES], z_scr[rows[c], cq + LANES:cq + 2 * LANES], low)
        return _swa_probs_t(k2_scr[g, keys[c], :], qm, bias_ref[tb, g], sink_ref[g])

    p_all = [[probs(c, g) for g in groups] for c in chunks]
    intra = [_hgrn_intra(*gates[c][:4], tril_mask) for c in chunks]

    st = [st_scr[h] for h in range(HG_HEADS)]
    for c in chunks:
        qg, _, _, v, eGL = gates[c]
        A, U = intra[c]
        o_h = _hgrn_out(A, qg, v, [s.astype(BF16) for s in st], z(c, C_HG), gnorm)
        st = [st[h] * eGL[:, _head(h)] + U[h] for h in range(HG_HEADS)]
        mix_ref[0, rows[c], 0:HG_WIDTH] = o_h.astype(BF16)
    for h in range(HG_HEADS):
        st_scr[h] = st[h]

    for c in chunks:
        par = c % 2
        win = slice((c - par) * CHUNK, (c - par + BAND) * CHUNK)
        for g in groups:
            b0, b1 = _swa_out_t(vt_scr[par, g, :, win], p_all[c][g], low)
            co = HG_WIDTH + g * ATTN_GROUP * ATTN_HEAD_DIM
            mix_ref[0, rows[c], co:co + LANES] = b0.astype(BF16)
            mix_ref[0, rows[c], co + LANES:co + 2 * LANES] = b1.astype(BF16)

    @pl.when(t == nt - 1)
    def _():
        w = min(WINDOW, tt)
        kp_ref[0] = z_scr[tt - w:tt, C_AK:C_AK + LANES]
        vp_ref[0] = z_scr[tt - w:tt, C_AV:C_AV + LANES]
        for h in range(HG_HEADS):
            sp_ref[0, h] = st_scr[h].T


def _alibi_tables(lq, lk, q0, with_chunk_mask):
    slopes = 2.0 ** (-8.0 * np.arange(1, ATTN_HEADS + 1) / ATTN_HEADS)
    dist = np.abs((q0 + np.arange(lq))[:, None] - np.arange(lk)[None, :]).astype(np.float64)
    base = -slopes.reshape(ATTN_KV_HEADS, ATTN_GROUP, 1, 1) * dist
    base = base.reshape(ATTN_KV_HEADS, ATTN_GROUP * lq, lk)
    if not with_chunk_mask:
        return jnp.asarray(base[None], F32)
    tabs = []
    for tb in range(BAND):
        valid = (np.arange(lk) // CHUNK) >= (BAND - 1 - tb)
        tabs.append(np.where(valid[None, None, :], base, NEG))
    return jnp.asarray(np.stack(tabs), F32)


def _sink_rows(sinks, lq):
    return jnp.repeat(sinks.astype(F32).reshape(ATTN_KV_HEADS, ATTN_GROUP), lq, axis=1)[..., None]


def _resident(shape):
    nd = len(shape)
    return pl.BlockSpec(shape, lambda *_: (0,) * nd, pipeline_mode=pl.Buffered(1))


def _mixers_prompt(x, w_in, lbl, gnorm, sinks, layer, tt):
    B, T, D = x.shape
    nt = T // tt
    hist = (BAND - 1) * CHUNK
    w = min(WINDOW, T)
    bias = jnp.swapaxes(_alibi_tables(CHUNK, BAND * CHUNK, hist, True), -1, -2)
    sink = jnp.swapaxes(_sink_rows(sinks, CHUNK), -1, -2)
    kern = functools.partial(_mixer_prompt_kernel, layer, tt)
    return pl.pallas_call(
        kern,
        grid=(B, nt),
        in_specs=[
            pl.BlockSpec((1, tt, D), lambda b, t: (b, t, 0)),
            _resident(w_in.shape),
            _resident(lbl.shape),
            _resident(gnorm.shape),
            _resident(bias.shape),
            _resident(sink.shape),
        ],
        out_specs=[
            pl.BlockSpec((1, tt, D), lambda b, t: (b, t, 0)),
            pl.BlockSpec((1, w, LANES), lambda b, t: (b, 0, 0)),
            pl.BlockSpec((1, w, LANES), lambda b, t: (b, 0, 0)),
            pl.BlockSpec((1, HG_HEADS, HG_DK, HG_DV), lambda b, t: (b, 0, 0, 0)),
        ],
        out_shape=[
            jax.ShapeDtypeStruct((B, T, D), BF16),
            jax.ShapeDtypeStruct((B, w, LANES), F32),
            jax.ShapeDtypeStruct((B, w, LANES), F32),
            jax.ShapeDtypeStruct((B, HG_HEADS, HG_DK, HG_DV), F32),
        ],
        scratch_shapes=[
            pltpu.VMEM((tt, IN_COLS), F32),
            pltpu.VMEM((hist, LANES), F32),
            pltpu.VMEM((hist, LANES), F32),
            pltpu.VMEM((ATTN_KV_HEADS, hist + tt, LANES), BF16),
            pltpu.VMEM((2, ATTN_KV_HEADS, LANES, hist + tt), BF16),
            pltpu.VMEM((HG_HEADS, HG_DV, HG_DK), F32),
        ],
        compiler_params=pltpu.CompilerParams(
            dimension_semantics=("arbitrary", "arbitrary"), vmem_limit_bytes=VMEM_LIMIT),
        name="mixers_prompt",
    )(x, w_in, lbl, gnorm, bias, sink)


def _mixer_sample_kernel(layer, nb, ts, x_ref, w_in_ref, lbl_ref, gnorm_ref, bias_ref, sink_ref,
                         kc_ref, vc_ref, s0_ref, mix_ref, kn_ref, vn_ref, sn_ref, z_scr):
    wlen = kc_ref.shape[1]
    z_scr[...] = _dot(x_ref[...].astype(BF16), w_in_ref[...])
    kn_ref[...] = z_scr[:, C_AK:C_AK + LANES]
    vn_ref[...] = z_scr[:, C_AV:C_AV + LANES]

    lb = _lower_bound(lbl_ref[...], layer)
    gnorm = gnorm_ref[...]
    tril_mask, tril3 = _tril_consts(ts)
    low = lax.broadcasted_iota(jnp.int32, (ts, LANES), 1) < ATTN_HEAD_DIM
    low_k = lax.broadcasted_iota(jnp.int32, (wlen + ts, LANES), 1) < ATTN_HEAD_DIM

    def batch_body(b, carry):
        r0 = pl.multiple_of(b * ts, ts)
        rows = pl.ds(r0, ts)
        st = [s0_ref[b, h].T for h in range(HG_HEADS)]
        qg, kg, kd, v, eGL = _hgrn_gates(z_scr[rows, C_HQ:C_HQ + HG_WIDTH], z_scr[rows, C_HF:C_HF + HG_WIDTH],
                                         z_scr[rows, C_HI:C_HI + HG_WIDTH], lb, tril3)
        A, U = _hgrn_intra(qg, kg, kd, v, tril_mask)
        o_h = _hgrn_out(A, qg, v, [s.astype(BF16) for s in st], z_scr[rows, C_HG:C_HG + HG_WIDTH], gnorm)
        for h in range(HG_HEADS):
            sn_ref[b, h] = (st[h] * eGL[:, _head(h)] + U[h]).T
        mix_ref[rows, 0:HG_WIDTH] = o_h.astype(BF16)

        k_all = jnp.concatenate([kc_ref[b], z_scr[rows, C_AK:C_AK + LANES]], axis=0)
        v_all = jnp.concatenate([vc_ref[b], z_scr[rows, C_AV:C_AV + LANES]], axis=0)
        k2 = _dup_halves(k_all, low_k)
        v2 = _dup_halves(v_all, low_k)
        for g in range(ATTN_KV_HEADS):
            cq = C_AQ + g * ATTN_GROUP * ATTN_HEAD_DIM
            s = _swa_scores(z_scr[rows, cq:cq + LANES], z_scr[rows, cq + LANES:cq + 2 * LANES],
                            k2[g].astype(BF16), bias_ref[0, g], low)
            b0, b1 = _swa_out(*_swa_probs(s, sink_ref[g]), v2[g].astype(BF16), low)
            co = HG_WIDTH + g * ATTN_GROUP * ATTN_HEAD_DIM
            mix_ref[rows, co:co + LANES] = b0.astype(BF16)
            mix_ref[rows, co + LANES:co + 2 * LANES] = b1.astype(BF16)
        return carry

    lax.fori_loop(0, nb, batch_body, 0)


def _mixers_sample(x, w_in, lbl, gnorm, sinks, k_cache, v_cache, s0, layer):
    nb, ts, D = x.shape
    wlen = k_cache.shape[1]
    bias = _alibi_tables(ts, wlen + ts, wlen, False)
    sink = _sink_rows(sinks, ts)
    kern = functools.partial(_mixer_sample_kernel, layer, nb, ts)
    n = nb * ts
    return pl.pallas_call(
        kern,
        out_shape=[
            jax.ShapeDtypeStruct((n, D), BF16),
            jax.ShapeDtypeStruct((n, LANES), F32),
            jax.ShapeDtypeStruct((n, LANES), F32),
            jax.ShapeDtypeStruct((nb, HG_HEADS, HG_DK, HG_DV), F32),
        ],
        scratch_shapes=[pltpu.VMEM((n, IN_COLS), F32)],
        compiler_params=pltpu.CompilerParams(vmem_limit_bytes=VMEM_LIMIT),
        name="mixers_sample",
    )(x.reshape(n, D), w_in, lbl, gnorm, bias, sink,
      k_cache.reshape(nb, wlen, LANES), v_cache.reshape(nb, wlen, LANES), s0)


def _mem_kv_kernel(d, m_ref, w_ref, k_ref, v_ref, kb_ref, vb_ref):
    r = _dot(m_ref[...].astype(BF16), w_ref[...])
    k_ref[...] = r[:, :d]
    v_ref[...] = r[:, d:]
    kb_ref[...] = r[:, :d].astype(BF16)
    vb_ref[...] = r[:, d:].astype(BF16)


def _mem_kv(mem, w_kv, tm):
    n, d = mem.shape
    row = lambda i: (i, 0)
    return pl.pallas_call(
        functools.partial(_mem_kv_kernel, d),
        grid=(n // tm,),
        in_specs=[pl.BlockSpec((tm, d), row), _resident(w_kv.shape)],
        out_specs=[pl.BlockSpec((tm, d), row)] * 4,
        out_shape=[jax.ShapeDtypeStruct((n, d), F32)] * 2 + [jax.ShapeDtypeStruct((n, d), BF16)] * 2,
        compiler_params=pltpu.CompilerParams(
            dimension_semantics=("arbitrary",), vmem_limit_bytes=VMEM_LIMIT),
        name="mem_kv",
    )(mem, w_kv)


def _post_kernel(alpha, nb, tb, x_ref, mix_ref, mk_ref, mv_ref, w_out_ref, w_q_ref, w_o_ref,
                 w_fi_ref, w_fo_ref, g_ref, b_ref, y_ref, o_scr, act_scr):
    d = x_ref.shape[-1]
    n = nb * tb
    dff = w_fo_ref.shape[0]
    hd = d // MEM_HEADS
    x = x_ref[...].reshape(n, d)
    mix = mix_ref[...].reshape(n, d)

    x1 = _layer_norm(alpha * x + _dot(mix, w_out_ref[...]), g_ref[0:1, :], b_ref[0:1, :])

    q = _dot(x1.astype(BF16), w_q_ref[...]).astype(BF16)
    for i in range(nb):
        rows = slice(i * tb, (i + 1) * tb)
        for h in range(MEM_HEADS):
            cols = slice(h * hd, (h + 1) * hd)
            s = _dot_nt(q[rows, cols], mk_ref[i, :, cols].astype(BF16)) * (hd ** -0.5)
            m = jnp.max(s, axis=-1, keepdims=True)
            p = jnp.exp(s - m)
            den = jnp.sum(p, axis=-1, keepdims=True)
            o = _dot(p.astype(BF16), mv_ref[i, :, cols].astype(BF16)) / den
            o_scr[rows, cols] = o.astype(BF16)
    x2 = _layer_norm(alpha * x1 + _dot(o_scr[...], w_o_ref[...]), g_ref[1:2, :], b_ref[1:2, :])

    x2b = x2.astype(BF16)
    for j in range(dff // MXU_COLS):
        cg = slice(j * MXU_COLS, (j + 1) * MXU_COLS)
        cu = slice(dff + j * MXU_COLS, dff + (j + 1) * MXU_COLS)
        gte = _dot(x2b, w_fi_ref[:, cg])
        up = _dot(x2b, w_fi_ref[:, cu])
        act_scr[:, cg] = (gte * _sigmoid(gte) * up).astype(BF16)
    x3 = _layer_norm(alpha * x2 + _dot(act_scr[...], w_fo_ref[...]), g_ref[2:3, :], b_ref[2:3, :])
    y_ref[...] = x3.reshape(nb, tb, d)


def _post_blocks(x, mix, mk, mv, w_out, w_q, w_o, w_fi, w_fo, ln_g, ln_b, alpha, nb, tb):
    B, T, d = x.shape
    dff = w_fo.shape[0]
    m_tok = mk.shape[1]
    n = nb * tb
    tile = lambda b, t: (b, t, 0)
    per_b = lambda b, t: (b, 0, 0)
    kern = functools.partial(_post_kernel, alpha, nb, tb)
    return pl.pallas_call(
        kern,
        grid=(B // nb, T // tb),
        in_specs=[
            pl.BlockSpec((nb, tb, d), tile),
            pl.BlockSpec((nb, tb, d), tile),
            pl.BlockSpec((nb, m_tok, d), per_b),
            pl.BlockSpec((nb, m_tok, d), per_b),
            _resident(w_out.shape), _resident(w_q.shape), _resident(w_o.shape),
            _resident(w_fi.shape), _resident(w_fo.shape),
            _resident(ln_g.shape), _resident(ln_b.shape),
        ],
        out_specs=pl.BlockSpec((nb, tb, d), tile),
        out_shape=jax.ShapeDtypeStruct((B, T, d), F32),
        scratch_shapes=[pltpu.VMEM((n, d), BF16), pltpu.VMEM((n, dff), BF16)],
        compiler_params=pltpu.CompilerParams(
            dimension_semantics=("arbitrary", "arbitrary"), vmem_limit_bytes=VMEM_LIMIT),
        name="post_blocks",
    )(x, mix, mk, mv, w_out, w_q, w_o, w_fi, w_fo, ln_g, ln_b)


TT_PROMPT = 512
TM_POST = 512
TM_MEM = 512


def kernel(x_prompt, x_sample, cache_swa_k, cache_swa_v, state_hgrn, cache_mem_k, cache_mem_v, mem_prompt, w_in, hgrn_lb_logits, hgrn_norm_g, attn_sinks, w_out, w_mem_q, w_mem_kv, w_mem_o, w_ffn_in, w_ffn_out, ln_g, ln_b):
    depth = w_in.shape[0]
    alpha = (2.0 * depth) ** 0.25
    B, T, D = x_prompt.shape
    nbs, ts, _ = x_sample.shape
    m_tok = mem_prompt.shape[1]
    lbl = hgrn_lb_logits.astype(F32)

    yp, ys = x_prompt, x_sample
    outs = [[] for _ in range(8)]
    for l in range(depth):
        w_in_l = w_in[l].astype(BF16)
        w_out_l = w_out[l].astype(BF16)
        w_q_l = w_mem_q[l].astype(BF16)
        w_kv_l = w_mem_kv[l].astype(BF16)
        w_o_l = w_mem_o[l].astype(BF16)
        w_fi_l = w_ffn_in[l].astype(BF16)
        w_fo_l = w_ffn_out[l].astype(BF16)
        gnorm = hgrn_norm_g[l].reshape(1, HG_DV).astype(F32)

        mix_p, k_p, v_p, s_p = _mixers_prompt(yp, w_in_l, lbl, gnorm, attn_sinks[l], l, TT_PROMPT)
        mk_p, mv_p, mk_b, mv_b = _mem_kv(mem_prompt.reshape(B * m_tok, D), w_kv_l, TM_MEM)
        yp = _post_blocks(yp, mix_p, mk_b.reshape(B, m_tok, D), mv_b.reshape(B, m_tok, D),
                          w_out_l, w_q_l, w_o_l, w_fi_l, w_fo_l, ln_g[l], ln_b[l], alpha, 1, TM_POST)

        mix_s, k_s, v_s, s_s = _mixers_sample(ys, w_in_l, lbl, gnorm, attn_sinks[l],
                                              cache_swa_k[l], cache_swa_v[l], state_hgrn[l], l)
        ys = _post_blocks(ys, mix_s.reshape(nbs, ts, D), cache_mem_k[l].reshape(nbs, m_tok, D),
                          cache_mem_v[l].reshape(nbs, m_tok, D),
                          w_out_l, w_q_l, w_o_l, w_fi_l, w_fo_l, ln_g[l], ln_b[l], alpha, nbs, ts)

        wlen = k_p.shape[1]
        new = [k_p.reshape(B, wlen, ATTN_KV_HEADS, ATTN_HEAD_DIM),
               v_p.reshape(B, wlen, ATTN_KV_HEADS, ATTN_HEAD_DIM),
               s_p,
               mk_p.reshape(B, m_tok, MEM_HEADS, D // MEM_HEADS),
               mv_p.reshape(B, m_tok, MEM_HEADS, D // MEM_HEADS),
               k_s.reshape(nbs, ts, ATTN_KV_HEADS, ATTN_HEAD_DIM),
               v_s.reshape(nbs, ts, ATTN_KV_HEADS, ATTN_HEAD_DIM),
               s_s]
        for acc, a in zip(outs, new):
            acc.append(a)
    return (yp, ys) + tuple(jnp.stack(a) for a in outs)
```

```python
import functools

import numpy as np
import jax
import jax.numpy as jnp
from jax import lax
from jax.experimental import pallas as pl
from jax.experimental.pallas import tpu as pltpu

F32 = jnp.float32
BF16 = jnp.bfloat16

CHUNK = 64
HG_HEADS = 4
HG_DK = 128
HG_DV = 128
HG_WIDTH = HG_HEADS * HG_DV
ATTN_HEADS = 8
ATTN_KV_HEADS = 2
ATTN_HEAD_DIM = 64
ATTN_GROUP = ATTN_HEADS // ATTN_KV_HEADS
ATTN_WIDTH = ATTN_HEADS * ATTN_HEAD_DIM
WINDOW = 128
BAND = WINDOW // CHUNK + 1
MEM_HEADS = 4
NEG = -1e30

C_HQ, C_HF, C_HI, C_HG = 0, 512, 1024, 1536
C_AQ = 2048
C_AK = C_AQ + ATTN_WIDTH
C_AV = C_AK + ATTN_KV_HEADS * ATTN_HEAD_DIM
IN_COLS = C_AV + ATTN_KV_HEADS * ATTN_HEAD_DIM

LANES = 128
MXU_COLS = 256
VMEM_LIMIT = 56 * 1024 * 1024


def _sigmoid(x):
    return 1.0 / (1.0 + jnp.exp(-x))


def _dot(a, b):
    return jnp.dot(a, b, preferred_element_type=F32)


def _dot_nt(a, b):
    return lax.dot_general(a, b, (((1,), (1,)), ((), ())), preferred_element_type=F32)


def _dot_tn(a, b):
    return lax.dot_general(a, b, (((0,), (0,)), ((), ())), preferred_element_type=F32)


def _layer_norm(y, g, b):
    mu = jnp.mean(y, axis=-1, keepdims=True)
    d = y - mu
    var = jnp.mean(d * d, axis=-1, keepdims=True)
    return d * lax.rsqrt(var + 1e-5) * g + b


def _lower_bound(lbl, layer):
    n = lbl.shape[0]
    rows = [lbl[i:i + 1, :] for i in range(n)]
    m = functools.reduce(jnp.maximum, rows)
    e = [jnp.exp(r - m) for r in rows]
    tot = functools.reduce(jnp.add, e)
    return functools.reduce(jnp.add, e[:layer + 1]) / tot


def _tril_consts(L):
    r = lax.broadcasted_iota(jnp.int32, (L, L), 0)
    c = lax.broadcasted_iota(jnp.int32, (L, L), 1)
    mask = c <= r
    tril = jnp.where(mask, 1.0, 0.0).astype(BF16)
    return mask, jnp.concatenate([tril, tril, tril], axis=1)


def _hgrn_gates(hq, hf, hi, lb, tril3):
    L = hq.shape[0]
    f = lb + (1.0 - lb) * _sigmoid(hf)
    logf = jnp.log(f)
    p0 = logf.astype(BF16)
    r0 = logf - p0.astype(F32)
    p1 = r0.astype(BF16)
    p2 = (r0 - p1.astype(F32)).astype(BF16)
    G = _dot(tril3, jnp.concatenate([p0, p1, p2], axis=0))
    eGL = jnp.exp(G[L - 1:L, :])
    qg = hq * _sigmoid(hq) * jnp.exp(G)
    kg = (1.0 - f) * jnp.exp(-G)
    return qg.astype(BF16), kg.astype(BF16), (kg * eGL).astype(BF16), hi.astype(BF16), eGL


def _head(h):
    return slice(h * HG_DK, (h + 1) * HG_DK)


def _hgrn_intra(qg, kg, kd, v, tril_mask):
    A = [jnp.where(tril_mask, _dot_nt(qg[:, _head(h)], kg[:, _head(h)]), 0.0).astype(BF16)
         for h in range(HG_HEADS)]
    U = [_dot_tn(v[:, _head(h)], kd[:, _head(h)]) for h in range(HG_HEADS)]
    return A, U


def _hgrn_out(A, qg, v, st_bf, hg, gnorm):
    gate = hg * _sigmoid(hg)
    outs = []
    for h in range(HG_HEADS):
        o = _dot(A[h], v[:, _head(h)]) + _dot_nt(qg[:, _head(h)], st_bf[h])
        ms = jnp.mean(o * o, axis=-1, keepdims=True)
        outs.append(o * lax.rsqrt(ms + 1e-6) * gnorm * gate[:, _head(h)])
    return jnp.concatenate(outs, axis=1)


def _dup_halves(a, low):
    sw = pltpu.roll(a, ATTN_HEAD_DIM, axis=1)
    return jnp.where(low, a, sw), jnp.where(low, sw, a)


def _swa_scores(qb0, qb1, k2, bias, low):
    return _dot_nt(_stack_q(qb0, qb1, low), k2) * (ATTN_HEAD_DIM ** -0.5) + bias


def _swa_probs(s, sink):
    m = jnp.maximum(jnp.max(s, axis=-1, keepdims=True), sink)
    p = jnp.exp(s - m)
    return p.astype(BF16), jnp.sum(p, axis=-1, keepdims=True) + jnp.exp(sink - m)


def _swa_out(p, den, v2, low):
    Lq = p.shape[0] // ATTN_GROUP
    o = _dot(p, v2) / den
    return (jnp.where(low, o[0:Lq], o[Lq:2 * Lq]), jnp.where(low, o[2 * Lq:3 * Lq], o[3 * Lq:4 * Lq]))


def _stack_q(qb0, qb1, low):
    return jnp.concatenate([jnp.where(low, qb0, 0.0), jnp.where(low, 0.0, qb0),
                            jnp.where(low, qb1, 0.0), jnp.where(low, 0.0, qb1)], axis=0).astype(BF16)


def _swa_probs_t(k2, qm, bias_t, sink_t):
    s = _dot_nt(k2, qm) * (ATTN_HEAD_DIM ** -0.5) + bias_t
    m = jnp.maximum(jnp.max(s, axis=0, keepdims=True), sink_t)
    e = jnp.exp(s - m)
    den = jnp.sum(e, axis=0, keepdims=True) + jnp.exp(sink_t - m)
    return (e * (1.0 / den)).astype(BF16)


def _swa_out_t(vt2, p_t, low):
    Lq = p_t.shape[1] // ATTN_GROUP
    o_t = _dot(vt2, p_t)
    a0 = o_t[:, 0:2 * Lq].T
    a1 = o_t[:, 2 * Lq:4 * Lq].T
    return jnp.where(low, a0[0:Lq], a0[Lq:2 * Lq]), jnp.where(low, a1[0:Lq], a1[Lq:2 * Lq])


def _mixer_prompt_kernel(layer, tt, nt, x0_ref, xa_ref, xb_ref, w_in_ref, lbl_ref, gnorm_ref, bias_ref,
                         sink_ref, mix_ref, kp_ref, vp_ref, sp_ref,
                         za_scr, zb_scr, xc_scr, kh_scr, vh_scr, k2_scr, vt_scr, st_scr):
    k = pl.program_id(0)
    t0 = lax.rem(2 * k, nt)

    @pl.when(k == 0)
    def _():
        za_scr[...] = _dot(x0_ref[0].astype(BF16), w_in_ref[...])

    @pl.when(t0 == 0)
    def _():
        st_scr[...] = jnp.zeros_like(st_scr)
        kh_scr[...] = jnp.zeros_like(kh_scr)
        vh_scr[...] = jnp.zeros_like(vh_scr)

    consts = (_lower_bound(lbl_ref[...], layer), gnorm_ref[...])
    scr = (kh_scr, vh_scr, k2_scr, vt_scr, st_scr)

    _mix_tile(tt, za_scr, t0, 0, consts, bias_ref, sink_ref, mix_ref, scr,
              _proj_pieces(xa_ref, xc_scr.at[0], w_in_ref, zb_scr))
    _mix_tile(tt, zb_scr, t0 + 1, tt, consts, bias_ref, sink_ref, mix_ref, scr,
              _proj_pieces(xb_ref, xc_scr.at[1], w_in_ref, za_scr))

    @pl.when(t0 + 1 == nt - 1)
    def _():
        w = min(WINDOW, tt)
        kp_ref[0] = zb_scr[tt - w:tt, C_AK:C_AK + LANES]
        vp_ref[0] = zb_scr[tt - w:tt, C_AV:C_AV + LANES]
        for h in range(HG_HEADS):
            sp_ref[0, h] = st_scr[h].T


def _proj_pieces(x_ref, xb_scr, w_in_ref, z_dst):
    def piece(j):
        cols = slice(j * MXU_COLS, (j + 1) * MXU_COLS)
        def run():
            if j == 0:
                xb_scr[...] = x_ref[0].astype(BF16)
            z_dst[:, cols] = _dot(xb_scr[...], w_in_ref[:, cols])
        return run
    return [piece(j) for j in range(IN_COLS // MXU_COLS)]


def _mix_tile(tt, z_scr, t, row0, consts, bias_ref, sink_ref, mix_ref, scr, next_proj):
    lb, gnorm = consts
    kh_scr, vh_scr, k2_scr, vt_scr, st_scr = scr
    n_chunks = tt // CHUNK
    hist = (BAND - 1) * CHUNK

    k_tile = z_scr[:, C_AK:C_AK + LANES]
    low_k = lax.broadcasted_iota(jnp.int32, (hist + tt, LANES), 1) < ATTN_HEAD_DIM
    ka, kb = _dup_halves(jnp.concatenate([kh_scr[...], k_tile], axis=0), low_k)
    k2_scr[0] = ka.astype(BF16)
    k2_scr[1] = kb.astype(BF16)
    v_tile = z_scr[:, C_AV:C_AV + LANES]
    v_hist = vh_scr[...]
    v_shift = (jnp.concatenate([v_hist, v_tile], axis=0),
               jnp.concatenate([v_hist[CHUNK:], v_tile, v_tile[tt - CHUNK:]], axis=0))
    for par in range(2):
        vt = v_shift[par].T.astype(BF16)
        for g in range(ATTN_KV_HEADS):
            blk = vt[g * ATTN_HEAD_DIM:(g + 1) * ATTN_HEAD_DIM]
            vt_scr[par, g] = jnp.concatenate([blk, blk], axis=0)
    kh_scr[...] = k_tile[tt - hist:]
    vh_scr[...] = v_tile[tt - hist:]

    tril_mask, tril3 = _tril_consts(CHUNK)
    low = lax.broadcasted_iota(jnp.int32, (CHUNK, LANES), 1) < ATTN_HEAD_DIM
    chunks = range(n_chunks)
    groups = range(ATTN_KV_HEADS)
    rows = [slice(c * CHUNK, (c + 1) * CHUNK) for c in chunks]
    out_rows = [slice(row0 + c * CHUNK, row0 + (c + 1) * CHUNK) for c in chunks]
    keys = [slice(c * CHUNK, (c + BAND) * CHUNK) for c in chunks]
    z = lambda c, col: z_scr[rows[c], col:col + HG_WIDTH]

    pieces = list(next_proj)
    n_pieces, n_slots, slot = len(pieces), 4 * n_chunks, [0]

    def emit_proj():
        slot[0] += 1
        while pieces and (n_pieces - len(pieces)) * n_slots < slot[0] * n_pieces:
            pieces.pop(0)()

    gates = []
    for c in chunks:
        gates.append(_hgrn_gates(z(c, C_HQ), z(c, C_HF), z(c, C_HI), lb, tril3))
        emit_proj()

    def probs(c, g):
        tb = BAND - 1 if c >= BAND - 1 else jnp.minimum(t * n_chunks + c, BAND - 1)
        cq = C_AQ + g * ATTN_GROUP * ATTN_HEAD_DIM
        qm = _stack_q(z_scr[rows[c], cq:cq + LANES], z_scr[rows[c], cq + LANES:cq + 2 * LANES], low)
        return _swa_probs_t(k2_scr[g, keys[c], :], qm, bias_ref[tb, g], sink_ref[g])

    p_all = []
    for c in chunks:
        p_all.append([probs(c, g) for g in groups])
        emit_proj()
    intra = [_hgrn_intra(*gates[c][:4], tril_mask) for c in chunks]

    st = [st_scr[h] for h in range(HG_HEADS)]
    for c in chunks:
        qg, _, _, v, eGL = gates[c]
        A, U = intra[c]
        o_h = _hgrn_out(A, qg, v, [s.astype(BF16) for s in st], z(c, C_HG), gnorm)
        st = [st[h] * eGL[:, _head(h)] + U[h] for h in range(HG_HEADS)]
        mix_ref[0, out_rows[c], 0:HG_WIDTH] = o_h.astype(BF16)
        emit_proj()
    for h in range(HG_HEADS):
        st_scr[h] = st[h]

    for c in chunks:
        par = c % 2
        win = slice((c - par) * CHUNK, (c - par + BAND) * CHUNK)
        for g in groups:
            b0, b1 = _swa_out_t(vt_scr[par, g, :, win], p_all[c][g], low)
            co = HG_WIDTH + g * ATTN_GROUP * ATTN_HEAD_DIM
            mix_ref[0, out_rows[c], co:co + LANES] = b0.astype(BF16)
            mix_ref[0, out_rows[c], co + LANES:co + 2 * LANES] = b1.astype(BF16)
        emit_proj()
    assert not pieces


def _alibi_tables(lq, lk, q0, with_chunk_mask):
    slopes = 2.0 ** (-8.0 * np.arange(1, ATTN_HEADS + 1) / ATTN_HEADS)
    dist = np.abs((q0 + np.arange(lq))[:, None] - np.arange(lk)[None, :]).astype(np.float64)
    base = -slopes.reshape(ATTN_KV_HEADS, ATTN_GROUP, 1, 1) * dist
    base = base.reshape(ATTN_KV_HEADS, ATTN_GROUP * lq, lk)
    if not with_chunk_mask:
        return jnp.asarray(base[None], F32)
    tabs = []
    for tb in range(BAND):
        valid = (np.arange(lk) // CHUNK) >= (BAND - 1 - tb)
        tabs.append(np.where(valid[None, None, :], base, NEG))
    return jnp.asarray(np.stack(tabs), F32)


def _sink_rows(sinks, lq):
    return jnp.repeat(sinks.astype(F32).reshape(ATTN_KV_HEADS, ATTN_GROUP), lq, axis=1)[..., None]


def _resident(shape):
    nd = len(shape)
    return pl.BlockSpec(shape, lambda *_: (0,) * nd, pipeline_mode=pl.Buffered(1))


def _mixers_prompt(x, w_in, lbl, gnorm, sinks, layer, tt):
    B, T, D = x.shape
    nt = T // tt
    hist = (BAND - 1) * CHUNK
    w = min(WINDOW, T)
    bias = jnp.swapaxes(_alibi_tables(CHUNK, BAND * CHUNK, hist, True), -1, -2)
    sink = jnp.swapaxes(_sink_rows(sinks, CHUNK), -1, -2)
    kern = functools.partial(_mixer_prompt_kernel, layer, tt, nt)
    assert nt % 2 == 0
    n_tiles = B * nt
    half = nt // 2

    def tile(j):
        j = jnp.minimum(j, n_tiles - 1)
        return (j // nt, j % nt, 0)

    return pl.pallas_call(
        kern,
        grid=(n_tiles // 2,),
        in_specs=[
            pl.BlockSpec((1, tt, D), lambda k: (0, 0, 0), pipeline_mode=pl.Buffered(1)),
            pl.BlockSpec((1, tt, D), lambda k: tile(2 * k + 1)),
            pl.BlockSpec((1, tt, D), lambda k: tile(2 * k + 2)),
            _resident(w_in.shape),
            _resident(lbl.shape),
            _resident(gnorm.shape),
            _resident(bias.shape),
            _resident(sink.shape),
        ],
        out_specs=[
            pl.BlockSpec((1, 2 * tt, D), lambda k: (k // half, k % half, 0)),
            pl.BlockSpec((1, w, LANES), lambda k: (k // half, 0, 0)),
            pl.BlockSpec((1, w, LANES), lambda k: (k // half, 0, 0)),
            pl.BlockSpec((1, HG_HEADS, HG_DK, HG_DV), lambda k: (k // half, 0, 0, 0)),
        ],
        out_shape=[
            jax.ShapeDtypeStruct((B, T, D), BF16),
            jax.ShapeDtypeStruct((B, w, LANES), F32),
            jax.ShapeDtypeStruct((B, w, LANES), F32),
            jax.ShapeDtypeStruct((B, HG_HEADS, HG_DK, HG_DV), F32),
        ],
        scratch_shapes=[
            pltpu.VMEM((tt, IN_COLS), F32),
            pltpu.VMEM((tt, IN_COLS), F32),
            pltpu.VMEM((2, tt, D), BF16),
            pltpu.VMEM((hist, LANES), F32),
            pltpu.VMEM((hist, LANES), F32),
            pltpu.VMEM((ATTN_KV_HEADS, hist + tt, LANES), BF16),
            pltpu.VMEM((2, ATTN_KV_HEADS, LANES, hist + tt), BF16),
            pltpu.VMEM((HG_HEADS, HG_DV, HG_DK), F32),
        ],
        compiler_params=pltpu.CompilerParams(
            dimension_semantics=("arbitrary",), vmem_limit_bytes=VMEM_LIMIT),
        name="mixers_prompt",
    )(x, x, x, w_in, lbl, gnorm, bias, sink)


def _mixer_sample_kernel(layer, nb, ts, x_ref, w_in_ref, lbl_ref, gnorm_ref, bias_ref, sink_ref,
                         kc_ref, vc_ref, s0_ref, mix_ref, kn_ref, vn_ref, sn_ref, z_scr):
    wlen = kc_ref.shape[1]
    z_scr[...] = _dot(x_ref[...].astype(BF16), w_in_ref[...])
    kn_ref[...] = z_scr[:, C_AK:C_AK + LANES]
    vn_ref[...] = z_scr[:, C_AV:C_AV + LANES]

    lb = _lower_bound(lbl_ref[...], layer)
    gnorm = gnorm_ref[...]
    tril_mask, tril3 = _tril_consts(ts)
    low = lax.broadcasted_iota(jnp.int32, (ts, LANES), 1) < ATTN_HEAD_DIM
    low_k = lax.broadcasted_iota(jnp.int32, (wlen + ts, LANES), 1) < ATTN_HEAD_DIM

    def batch_body(b, carry):
        r0 = pl.multiple_of(b * ts, ts)
        rows = pl.ds(r0, ts)
        st = [s0_ref[b, h].T for h in range(HG_HEADS)]
        qg, kg, kd, v, eGL = _hgrn_gates(z_scr[rows, C_HQ:C_HQ + HG_WIDTH], z_scr[rows, C_HF:C_HF + HG_WIDTH],
                                         z_scr[rows, C_HI:C_HI + HG_WIDTH], lb, tril3)
        A, U = _hgrn_intra(qg, kg, kd, v, tril_mask)
        o_h = _hgrn_out(A, qg, v, [s.astype(BF16) for s in st], z_scr[rows, C_HG:C_HG + HG_WIDTH], gnorm)
        for h in range(HG_HEADS):
            sn_ref[b, h] = (st[h] * eGL[:, _head(h)] + U[h]).T
        mix_ref[rows, 0:HG_WIDTH] = o_h.astype(BF16)

        k_all = jnp.concatenate([kc_ref[b], z_scr[rows, C_AK:C_AK + LANES]], axis=0)
        v_all = jnp.concatenate([vc_ref[b], z_scr[rows, C_AV:C_AV + LANES]], axis=0)
        k2 = _dup_halves(k_all, low_k)
        v2 = _dup_halves(v_all, low_k)
        for g in range(ATTN_KV_HEADS):
            cq = C_AQ + g * ATTN_GROUP * ATTN_HEAD_DIM
            s = _swa_scores(z_scr[rows, cq:cq + LANES], z_scr[rows, cq + LANES:cq + 2 * LANES],
                            k2[g].astype(BF16), bias_ref[0, g], low)
            b0, b1 = _swa_out(*_swa_probs(s, sink_ref[g]), v2[g].astype(BF16), low)
            co = HG_WIDTH + g * ATTN_GROUP * ATTN_HEAD_DIM
            mix_ref[rows, co:co + LANES] = b0.astype(BF16)
            mix_ref[rows, co + LANES:co + 2 * LANES] = b1.astype(BF16)
        return carry

    lax.fori_loop(0, nb, batch_body, 0)


def _mixers_sample(x, w_in, lbl, gnorm, sinks, k_cache, v_cache, s0, layer):
    nb, ts, D = x.shape
    wlen = k_cache.shape[1]
    bias = _alibi_tables(ts, wlen + ts, wlen, False)
    sink = _sink_rows(sinks, ts)
    kern = functools.partial(_mixer_sample_kernel, layer, nb, ts)
    n = nb * ts
    return pl.pallas_call(
        kern,
        out_shape=[
            jax.ShapeDtypeStruct((n, D), BF16),
            jax.ShapeDtypeStruct((n, LANES), F32),
            jax.ShapeDtypeStruct((n, LANES), F32),
            jax.ShapeDtypeStruct((nb, HG_HEADS, HG_DK, HG_DV), F32),
        ],
        scratch_shapes=[pltpu.VMEM((n, IN_COLS), F32)],
        compiler_params=pltpu.CompilerParams(vmem_limit_bytes=VMEM_LIMIT),
        name="mixers_sample",
    )(x.reshape(n, D), w_in, lbl, gnorm, bias, sink,
      k_cache.reshape(nb, wlen, LANES), v_cache.reshape(nb, wlen, LANES), s0)


def _mem_kv_kernel(d, m_ref, w_ref, k_ref, v_ref, kb_ref, vb_ref):
    r = _dot(m_ref[...].astype(BF16), w_ref[...])
    k_ref[...] = r[:, :d]
    v_ref[...] = r[:, d:]
    kb_ref[...] = r[:, :d].astype(BF16)
    vb_ref[...] = r[:, d:].astype(BF16)


def _mem_kv(mem, w_kv, tm):
    n, d = mem.shape
    row = lambda i: (i, 0)
    return pl.pallas_call(
        functools.partial(_mem_kv_kernel, d),
        grid=(n // tm,),
        in_specs=[pl.BlockSpec((tm, d), row), _resident(w_kv.shape)],
        out_specs=[pl.BlockSpec((tm, d), row)] * 4,
        out_shape=[jax.ShapeDtypeStruct((n, d), F32)] * 2 + [jax.ShapeDtypeStruct((n, d), BF16)] * 2,
        compiler_params=pltpu.CompilerParams(
            dimension_semantics=("arbitrary",), vmem_limit_bytes=VMEM_LIMIT),
        name="mem_kv",
    )(mem, w_kv)


def _post_kernel(alpha, nb, tb, n_sub, x_ref, mix_ref, mk_ref, mv_ref, w_out_ref, w_q_ref, w_o_ref,
                 w_fi_ref, w_fo_ref, g_ref, b_ref, y_ref, o_scr, act_scr):
    d = x_ref.shape[-1]
    n = nb * tb
    ns = n // n_sub
    dff = w_fo_ref.shape[0]
    hd = d // MEM_HEADS
    subs = range(n_sub)
    rows = [slice(s * ns, (s + 1) * ns) for s in subs]
    x_all = x_ref[...].reshape(n, d)
    mix_all = mix_ref[...].reshape(n, d)

    x1 = [_layer_norm(alpha * x_all[rows[s]] + _dot(mix_all[rows[s]], w_out_ref[...]),
                      g_ref[0:1, :], b_ref[0:1, :]) for s in subs]
    q = [_dot(x1[s].astype(BF16), w_q_ref[...]).astype(BF16) for s in subs]

    seg = tb if tb < ns else ns
    pieces = [(slice(r0, r0 + seg), r0 // tb) for r0 in range(0, n, seg)]
    for h in range(MEM_HEADS):
        cols = slice(h * hd, (h + 1) * hd)
        for r, i in pieces:
            qh = q[r.start // ns][r.start % ns:r.start % ns + seg, cols]
            s = _dot_nt(qh, mk_ref[i, :, cols].astype(BF16)) * (hd ** -0.5)
            m = jnp.max(s, axis=-1, keepdims=True)
            p = jnp.exp(s - m)
            den = jnp.sum(p, axis=-1, keepdims=True)
            o = _dot(p.astype(BF16), mv_ref[i, :, cols].astype(BF16)) / den
            o_scr[r, cols] = o.astype(BF16)
    x2 = [_layer_norm(alpha * x1[s] + _dot(o_scr[rows[s], :], w_o_ref[...]),
                      g_ref[1:2, :], b_ref[1:2, :]) for s in subs]

    x2b = [x2[s].astype(BF16) for s in subs]
    for j in range(dff // MXU_COLS):
        cg = slice(j * MXU_COLS, (j + 1) * MXU_COLS)
        cu = slice(dff + j * MXU_COLS, dff + (j + 1) * MXU_COLS)
        for s in subs:
            gte = _dot(x2b[s], w_fi_ref[:, cg])
            up = _dot(x2b[s], w_fi_ref[:, cu])
            act_scr[rows[s], cg] = (gte * _sigmoid(gte) * up).astype(BF16)
    x3 = [_layer_norm(alpha * x2[s] + _dot(act_scr[rows[s], :], w_fo_ref[...]),
                      g_ref[2:3, :], b_ref[2:3, :]) for s in subs]
    y_ref[...] = jnp.concatenate(x3, axis=0).reshape(nb, tb, d)


def _post_blocks(x, mix, mk, mv, w_out, w_q, w_o, w_fi, w_fo, ln_g, ln_b, alpha, nb, tb, n_sub):
    B, T, d = x.shape
    dff = w_fo.shape[0]
    m_tok = mk.shape[1]
    n = nb * tb
    tile = lambda b, t: (b, t, 0)
    per_b = lambda b, t: (b, 0, 0)
    kern = functools.partial(_post_kernel, alpha, nb, tb, n_sub)
    return pl.pallas_call(
        kern,
        grid=(B // nb, T // tb),
        in_specs=[
            pl.BlockSpec((nb, tb, d), tile),
            pl.BlockSpec((nb, tb, d), tile),
            pl.BlockSpec((nb, m_tok, d), per_b),
            pl.BlockSpec((nb, m_tok, d), per_b),
            _resident(w_out.shape), _resident(w_q.shape), _resident(w_o.shape),
            _resident(w_fi.shape), _resident(w_fo.shape),
            _resident(ln_g.shape), _resident(ln_b.shape),
        ],
        out_specs=pl.BlockSpec((nb, tb, d), tile),
        out_shape=jax.ShapeDtypeStruct((B, T, d), F32),
        scratch_shapes=[pltpu.VMEM((n, d), BF16), pltpu.VMEM((n, dff), BF16)],
        compiler_params=pltpu.CompilerParams(
            dimension_semantics=("arbitrary", "arbitrary"), vmem_limit_bytes=VMEM_LIMIT),
        name="post_blocks",
    )(x, mix, mk, mv, w_out, w_q, w_o, w_fi, w_fo, ln_g, ln_b)


TT_PROMPT = 512
TM_POST = 512
SUB_POST = 2
TM_MEM = 512


def kernel(x_prompt, x_sample, cache_swa_k, cache_swa_v, state_hgrn, cache_mem_k, cache_mem_v, mem_prompt, w_in, hgrn_lb_logits, hgrn_norm_g, attn_sinks, w_out, w_mem_q, w_mem_kv, w_mem_o, w_ffn_in, w_ffn_out, ln_g, ln_b):
    depth = w_in.shape[0]
    alpha = (2.0 * depth) ** 0.25
    B, T, D = x_prompt.shape
    nbs, ts, _ = x_sample.shape
    m_tok = mem_prompt.shape[1]
    lbl = hgrn_lb_logits.astype(F32)

    yp, ys = x_prompt, x_sample
    outs = [[] for _ in range(8)]
    for l in range(depth):
        w_in_l = w_in[l].astype(BF16)
        w_out_l = w_out[l].astype(BF16)
        w_q_l = w_mem_q[l].astype(BF16)
        w_kv_l = w_mem_kv[l].astype(BF16)
        w_o_l = w_mem_o[l].astype(BF16)
        w_fi_l = w_ffn_in[l].astype(BF16)
        w_fo_l = w_ffn_out[l].astype(BF16)
        gnorm = hgrn_norm_g[l].reshape(1, HG_DV).astype(F32)

        mix_p, k_p, v_p, s_p = _mixers_prompt(yp, w_in_l, lbl, gnorm, attn_sinks[l], l, TT_PROMPT)
        mk_p, mv_p, mk_b, mv_b = _mem_kv(mem_prompt.reshape(B * m_tok, D), w_kv_l, TM_MEM)
        yp = _post_blocks(yp, mix_p, mk_b.reshape(B, m_tok, D), mv_b.reshape(B, m_tok, D),
                          w_out_l, w_q_l, w_o_l, w_fi_l, w_fo_l, ln_g[l], ln_b[l], alpha, 1, TM_POST, SUB_POST)

        mix_s, k_s, v_s, s_s = _mixers_sample(ys, w_in_l, lbl, gnorm, attn_sinks[l],
                                              cache_swa_k[l], cache_swa_v[l], state_hgrn[l], l)
        ys = _post_blocks(ys, mix_s.reshape(nbs, ts, D), cache_mem_k[l].reshape(nbs, m_tok, D),
                          cache_mem_v[l].reshape(nbs, m_tok, D),
                          w_out_l, w_q_l, w_o_l, w_fi_l, w_fo_l, ln_g[l], ln_b[l], alpha, nbs, ts, 1)

        wlen = k_p.shape[1]
        new = [k_p.reshape(B, wlen, ATTN_KV_HEADS, ATTN_HEAD_DIM),
               v_p.reshape(B, wlen, ATTN_KV_HEADS, ATTN_HEAD_DIM),
               s_p,
               mk_p.reshape(B, m_tok, MEM_HEADS, D // MEM_HEADS),
               mv_p.reshape(B, m_tok, MEM_HEADS, D // MEM_HEADS),
               k_s.reshape(nbs, ts, ATTN_KV_HEADS, ATTN_HEAD_DIM),
               v_s.reshape(nbs, ts, ATTN_KV_HEADS, ATTN_HEAD_DIM),
               s_s]
        for acc, a in zip(outs, new):
            acc.append(a)
    return (yp, ys) + tuple(jnp.stack(a) for a in outs)
```

```python
import functools

import numpy as np
import jax
import jax.numpy as jnp
from jax import lax
from jax.experimental import pallas as pl
from jax.experimental.pallas import tpu as pltpu

F32 = jnp.float32
BF16 = jnp.bfloat16

CHUNK = 64
HG_HEADS = 4
HG_DK = 128
HG_DV = 128
HG_WIDTH = HG_HEADS * HG_DV
ATTN_HEADS = 8
ATTN_KV_HEADS = 2
ATTN_HEAD_DIM = 64
ATTN_GROUP = ATTN_HEADS // ATTN_KV_HEADS
ATTN_WIDTH = ATTN_HEADS * ATTN_HEAD_DIM
WINDOW = 128
BAND = WINDOW // CHUNK + 1
MEM_HEADS = 4
NEG = -1e30

C_HQ, C_HF, C_HI, C_HG = 0, 512, 1024, 1536
C_AQ = 2048
C_AK = C_AQ + ATTN_WIDTH
C_AV = C_AK + ATTN_KV_HEADS * ATTN_HEAD_DIM
IN_COLS = C_AV + ATTN_KV_HEADS * ATTN_HEAD_DIM

LANES = 128
MXU_COLS = 256
VMEM_LIMIT = 56 * 1024 * 1024


def _sigmoid(x):
    return 1.0 / (1.0 + jnp.exp(-x))


def _dot(a, b):
    return jnp.dot(a, b, preferred_element_type=F32)


def _dot_nt(a, b):
    return lax.dot_general(a, b, (((1,), (1,)), ((), ())), preferred_element_type=F32)


def _dot_tn(a, b):
    return lax.dot_general(a, b, (((0,), (0,)), ((), ())), preferred_element_type=F32)


def _layer_norm(y, g, b):
    mu = jnp.mean(y, axis=-1, keepdims=True)
    d = y - mu
    var = jnp.mean(d * d, axis=-1, keepdims=True)
    return d * lax.rsqrt(var + 1e-5) * g + b


def _lower_bound(lbl, layer):
    n = lbl.shape[0]
    rows = [lbl[i:i + 1, :] for i in range(n)]
    m = functools.reduce(jnp.maximum, rows)
    e = [jnp.exp(r - m) for r in rows]
    tot = functools.reduce(jnp.add, e)
    return functools.reduce(jnp.add, e[:layer + 1]) / tot


def _tril_consts(L):
    r = lax.broadcasted_iota(jnp.int32, (L, L), 0)
    c = lax.broadcasted_iota(jnp.int32, (L, L), 1)
    mask = c <= r
    tril = jnp.where(mask, 1.0, 0.0).astype(BF16)
    return mask, jnp.concatenate([tril, tril, tril], axis=1)


def _hgrn_gates(hq, hf, hi, lb, tril3):
    L = hq.shape[0]
    f = lb + (1.0 - lb) * _sigmoid(hf)
    logf = jnp.log(f)
    p0 = logf.astype(BF16)
    r0 = logf - p0.astype(F32)
    p1 = r0.astype(BF16)
    p2 = (r0 - p1.astype(F32)).astype(BF16)
    G = _dot(tril3, jnp.concatenate([p0, p1, p2], axis=0))
    eGL = jnp.exp(G[L - 1:L, :])
    qg = hq * _sigmoid(hq) * jnp.exp(G)
    kg = (1.0 - f) * jnp.exp(-G)
    return qg.astype(BF16), kg.astype(BF16), (kg * eGL).astype(BF16), hi.astype(BF16), eGL


def _head(h):
    return slice(h * HG_DK, (h + 1) * HG_DK)


def _hgrn_intra(qg, kg, kd, v, tril_mask):
    A = [jnp.where(tril_mask, _dot_nt(qg[:, _head(h)], kg[:, _head(h)]), 0.0).astype(BF16)
         for h in range(HG_HEADS)]
    U = [_dot_tn(v[:, _head(h)], kd[:, _head(h)]) for h in range(HG_HEADS)]
    return A, U


def _hgrn_out(A, qg, v, st_bf, hg, gnorm):
    gate = hg * _sigmoid(hg)
    outs = []
    for h in range(HG_HEADS):
        o = _dot(A[h], v[:, _head(h)]) + _dot_nt(qg[:, _head(h)], st_bf[h])
        ms = jnp.mean(o * o, axis=-1, keepdims=True)
        outs.append(o * lax.rsqrt(ms + 1e-6) * gnorm * gate[:, _head(h)])
    return jnp.concatenate(outs, axis=1)


def _dup_halves(a, low):
    sw = pltpu.roll(a, ATTN_HEAD_DIM, axis=1)
    return jnp.where(low, a, sw), jnp.where(low, sw, a)


def _swa_scores(qb0, qb1, k2, bias, low):
    return _dot_nt(_stack_q(qb0, qb1, low), k2) * (ATTN_HEAD_DIM ** -0.5) + bias


def _swa_probs(s, sink):
    m = jnp.maximum(jnp.max(s, axis=-1, keepdims=True), sink)
    p = jnp.exp(s - m)
    return p.astype(BF16), jnp.sum(p, axis=-1, keepdims=True) + jnp.exp(sink - m)


def _swa_out(p, den, v2, low):
    Lq = p.shape[0] // ATTN_GROUP
    o = _dot(p, v2) / den
    return (jnp.where(low, o[0:Lq], o[Lq:2 * Lq]), jnp.where(low, o[2 * Lq:3 * Lq], o[3 * Lq:4 * Lq]))


def _stack_q(qb0, qb1, low):
    return jnp.concatenate([jnp.where(low, qb0, 0.0), jnp.where(low, 0.0, qb0),
                            jnp.where(low, qb1, 0.0), jnp.where(low, 0.0, qb1)], axis=0).astype(BF16)


def _swa_probs_t(k2, qm, bias_t, sink_t):
    s = _dot_nt(k2, qm) * (ATTN_HEAD_DIM ** -0.5) + bias_t
    m = jnp.maximum(jnp.max(s, axis=0, keepdims=True), sink_t)
    e = jnp.exp(s - m)
    den = jnp.sum(e, axis=0, keepdims=True) + jnp.exp(sink_t - m)
    return (e * (1.0 / den)).astype(BF16)


def _swa_out_t(vt2, p_t, low):
    Lq = p_t.shape[1] // ATTN_GROUP
    o_t = _dot(vt2, p_t)
    a0 = o_t[:, 0:2 * Lq].T
    a1 = o_t[:, 2 * Lq:4 * Lq].T
    return jnp.where(low, a0[0:Lq], a0[Lq:2 * Lq]), jnp.where(low, a1[0:Lq], a1[Lq:2 * Lq])


def _mixer_prompt_kernel(layer, tt, nt, x0_ref, xa_ref, xb_ref, w_in_ref, lbl_ref, gnorm_ref, bias_ref,
                         sink_ref, mix_ref, kp_ref, vp_ref, sp_ref,
                         za_scr, zb_scr, xc_scr, kh_scr, vh_scr, k2_scr, vt_scr, st_scr):
    k = pl.program_id(0)
    t0 = lax.rem(2 * k, nt)

    @pl.when(k == 0)
    def _():
        za_scr[...] = _dot(x0_ref[0].astype(BF16), w_in_ref[...])

    @pl.when(t0 == 0)
    def _():
        st_scr[...] = jnp.zeros_like(st_scr)
        kh_scr[...] = jnp.zeros_like(kh_scr)
        vh_scr[...] = jnp.zeros_like(vh_scr)

    consts = (_lower_bound(lbl_ref[...], layer), gnorm_ref[...])
    scr = (kh_scr, vh_scr, k2_scr, vt_scr, st_scr)

    _mix_tile(tt, za_scr, t0, 0, consts, bias_ref, sink_ref, mix_ref, scr,
              _proj_pieces(xa_ref, xc_scr.at[0], w_in_ref, zb_scr))
    _mix_tile(tt, zb_scr, t0 + 1, tt, consts, bias_ref, sink_ref, mix_ref, scr,
              _proj_pieces(xb_ref, xc_scr.at[1], w_in_ref, za_scr))

    @pl.when(t0 + 1 == nt - 1)
    def _():
        w = min(WINDOW, tt)
        kp_ref[0] = zb_scr[tt - w:tt, C_AK:C_AK + LANES]
        vp_ref[0] = zb_scr[tt - w:tt, C_AV:C_AV + LANES]
        for h in range(HG_HEADS):
            sp_ref[0, h] = st_scr[h].T


def _proj_pieces(x_ref, xb_scr, w_in_ref, z_dst):
    def piece(j):
        cols = slice(j * MXU_COLS, (j + 1) * MXU_COLS)
        def run():
            if j == 0:
                xb_scr[...] = x_ref[0].astype(BF16)
            z_dst[:, cols] = _dot(xb_scr[...], w_in_ref[:, cols])
        return run
    return [piece(j) for j in range(IN_COLS // MXU_COLS)]


def _mix_tile(tt, z_scr, t, row0, consts, bias_ref, sink_ref, mix_ref, scr, next_proj):
    lb, gnorm = consts
    kh_scr, vh_scr, k2_scr, vt_scr, st_scr = scr
    n_chunks = tt // CHUNK
    hist = (BAND - 1) * CHUNK

    k_tile = z_scr[:, C_AK:C_AK + LANES]
    low_k = lax.broadcasted_iota(jnp.int32, (hist + tt, LANES), 1) < ATTN_HEAD_DIM
    ka, kb = _dup_halves(jnp.concatenate([kh_scr[...], k_tile], axis=0), low_k)
    k2_scr[0] = ka.astype(BF16)
    k2_scr[1] = kb.astype(BF16)
    v_tile = z_scr[:, C_AV:C_AV + LANES]
    v_hist = vh_scr[...]
    v_shift = (jnp.concatenate([v_hist, v_tile], axis=0),
               jnp.concatenate([v_hist[CHUNK:], v_tile, v_tile[tt - CHUNK:]], axis=0))
    for par in range(2):
        vt = v_shift[par].T.astype(BF16)
        for g in range(ATTN_KV_HEADS):
            blk = vt[g * ATTN_HEAD_DIM:(g + 1) * ATTN_HEAD_DIM]
            vt_scr[par, g] = jnp.concatenate([blk, blk], axis=0)
    kh_scr[...] = k_tile[tt - hist:]
    vh_scr[...] = v_tile[tt - hist:]

    tril_mask, tril3 = _tril_consts(CHUNK)
    low = lax.broadcasted_iota(jnp.int32, (CHUNK, LANES), 1) < ATTN_HEAD_DIM
    chunks = range(n_chunks)
    groups = range(ATTN_KV_HEADS)
    rows = [slice(c * CHUNK, (c + 1) * CHUNK) for c in chunks]
    out_rows = [slice(row0 + c * CHUNK, row0 + (c + 1) * CHUNK) for c in chunks]
    keys = [slice(c * CHUNK, (c + BAND) * CHUNK) for c in chunks]
    z = lambda c, col: z_scr[rows[c], col:col + HG_WIDTH]

    pieces = list(next_proj)
    n_pieces, n_slots, slot = len(pieces), 4 * n_chunks, [0]

    def emit_proj():
        slot[0] += 1
        while pieces and (n_pieces - len(pieces)) * n_slots < slot[0] * n_pieces:
            pieces.pop(0)()

    gates = []
    for c in chunks:
        gates.append(_hgrn_gates(z(c, C_HQ), z(c, C_HF), z(c, C_HI), lb, tril3))
        emit_proj()

    def probs(c, g):
        tb = BAND - 1 if c >= BAND - 1 else jnp.minimum(t * n_chunks + c, BAND - 1)
        cq = C_AQ + g * ATTN_GROUP * ATTN_HEAD_DIM
        qm = _stack_q(z_scr[rows[c], cq:cq + LANES], z_scr[rows[c], cq + LANES:cq + 2 * LANES], low)
        return _swa_probs_t(k2_scr[g, keys[c], :], qm, bias_ref[tb, g], sink_ref[g])

    p_all = []
    for c in chunks:
        p_all.append([probs(c, g) for g in groups])
        emit_proj()
    intra = [_hgrn_intra(*gates[c][:4], tril_mask) for c in chunks]

    st = [st_scr[h] for h in range(HG_HEADS)]
    for c in chunks:
        qg, _, _, v, eGL = gates[c]
        A, U = intra[c]
        o_h = _hgrn_out(A, qg, v, [s.astype(BF16) for s in st], z(c, C_HG), gnorm)
        st = [st[h] * eGL[:, _head(h)] + U[h] for h in range(HG_HEADS)]
        mix_ref[0, out_rows[c], 0:HG_WIDTH] = o_h.astype(BF16)
        emit_proj()
    for h in range(HG_HEADS):
        st_scr[h] = st[h]

    for c in chunks:
        par = c % 2
        win = slice((c - par) * CHUNK, (c - par + BAND) * CHUNK)
        for g in groups:
            b0, b1 = _swa_out_t(vt_scr[par, g, :, win], p_all[c][g], low)
            co = HG_WIDTH + g * ATTN_GROUP * ATTN_HEAD_DIM
            mix_ref[0, out_rows[c], co:co + LANES] = b0.astype(BF16)
            mix_ref[0, out_rows[c], co + LANES:co + 2 * LANES] = b1.astype(BF16)
        emit_proj()
    assert not pieces


def _alibi_tables(lq, lk, q0, with_chunk_mask):
    slopes = 2.0 ** (-8.0 * np.arange(1, ATTN_HEADS + 1) / ATTN_HEADS)
    dist = np.abs((q0 + np.arange(lq))[:, None] - np.arange(lk)[None, :]).astype(np.float64)
    base = -slopes.reshape(ATTN_KV_HEADS, ATTN_GROUP, 1, 1) * dist
    base = base.reshape(ATTN_KV_HEADS, ATTN_GROUP * lq, lk)
    if not with_chunk_mask:
        return jnp.asarray(base[None], F32)
    tabs = []
    for tb in range(BAND):
        valid = (np.arange(lk) // CHUNK) >= (BAND - 1 - tb)
        tabs.append(np.where(valid[None, None, :], base, NEG))
    return jnp.asarray(np.stack(tabs), F32)


def _sink_rows(sinks, lq):
    return jnp.repeat(sinks.astype(F32).reshape(ATTN_KV_HEADS, ATTN_GROUP), lq, axis=1)[..., None]


def _resident(shape):
    nd = len(shape)
    return pl.BlockSpec(shape, lambda *_: (0,) * nd, pipeline_mode=pl.Buffered(1))


def _mixers_prompt(x, w_in, lbl, gnorm, sinks, layer, tt):
    B, T, D = x.shape
    nt = T // tt
    hist = (BAND - 1) * CHUNK
    w = min(WINDOW, T)
    bias = jnp.swapaxes(_alibi_tables(CHUNK, BAND * CHUNK, hist, True), -1, -2)
    sink = jnp.swapaxes(_sink_rows(sinks, CHUNK), -1, -2)
    kern = functools.partial(_mixer_prompt_kernel, layer, tt, nt)
    assert nt % 2 == 0
    n_tiles = B * nt
    half = nt // 2

    def tile(j):
        j = jnp.minimum(j, n_tiles - 1)
        return (j // nt, j % nt, 0)

    return pl.pallas_call(
        kern,
        grid=(n_tiles // 2,),
        in_specs=[
            pl.BlockSpec((1, tt, D), lambda k: (0, 0, 0), pipeline_mode=pl.Buffered(1)),
            pl.BlockSpec((1, tt, D), lambda k: tile(2 * k + 1)),
            pl.BlockSpec((1, tt, D), lambda k: tile(2 * k + 2)),
            _resident(w_in.shape),
            _resident(lbl.shape),
            _resident(gnorm.shape),
            _resident(bias.shape),
            _resident(sink.shape),
        ],
        out_specs=[
            pl.BlockSpec((1, 2 * tt, D), lambda k: (k // half, k % half, 0)),
            pl.BlockSpec((1, w, LANES), lambda k: (k // half, 0, 0)),
            pl.BlockSpec((1, w, LANES), lambda k: (k // half, 0, 0)),
            pl.BlockSpec((1, HG_HEADS, HG_DK, HG_DV), lambda k: (k // half, 0, 0, 0)),
        ],
        out_shape=[
            jax.ShapeDtypeStruct((B, T, D), BF16),
            jax.ShapeDtypeStruct((B, w, LANES), F32),
            jax.ShapeDtypeStruct((B, w, LANES), F32),
            jax.ShapeDtypeStruct((B, HG_HEADS, HG_DK, HG_DV), F32),
        ],
        scratch_shapes=[
            pltpu.VMEM((tt, IN_COLS), F32),
            pltpu.VMEM((tt, IN_COLS), F32),
            pltpu.VMEM((2, tt, D), BF16),
            pltpu.VMEM((hist, LANES), F32),
            pltpu.VMEM((hist, LANES), F32),
            pltpu.VMEM((ATTN_KV_HEADS, hist + tt, LANES), BF16),
            pltpu.VMEM((2, ATTN_KV_HEADS, LANES, hist + tt), BF16),
            pltpu.VMEM((HG_HEADS, HG_DV, HG_DK), F32),
        ],
        compiler_params=pltpu.CompilerParams(
            dimension_semantics=("arbitrary",), vmem_limit_bytes=VMEM_LIMIT),
        name="mixers_prompt",
    )(x, x, x, w_in, lbl, gnorm, bias, sink)


def _mixer_sample_kernel(layer, nb, ts, x_ref, w_in_ref, lbl_ref, gnorm_ref, bias_ref, sink_ref,
                         kc_ref, vc_ref, s0_ref, mix_ref, kn_ref, vn_ref, sn_ref, z_scr):
    wlen = kc_ref.shape[1]
    z_scr[...] = _dot(x_ref[...].astype(BF16), w_in_ref[...])
    kn_ref[...] = z_scr[:, C_AK:C_AK + LANES]
    vn_ref[...] = z_scr[:, C_AV:C_AV + LANES]

    lb = _lower_bound(lbl_ref[...], layer)
    gnorm = gnorm_ref[...]
    tril_mask, tril3 = _tril_consts(ts)
    low = lax.broadcasted_iota(jnp.int32, (ts, LANES), 1) < ATTN_HEAD_DIM
    low_k = lax.broadcasted_iota(jnp.int32, (wlen + ts, LANES), 1) < ATTN_HEAD_DIM

    def batch_body(b, carry):
        r0 = pl.multiple_of(b * ts, ts)
        rows = pl.ds(r0, ts)
        st = [s0_ref[b, h].T for h in range(HG_HEADS)]
        qg, kg, kd, v, eGL = _hgrn_gates(z_scr[rows, C_HQ:C_HQ + HG_WIDTH], z_scr[rows, C_HF:C_HF + HG_WIDTH],
                                         z_scr[rows, C_HI:C_HI + HG_WIDTH], lb, tril3)
        A, U = _hgrn_intra(qg, kg, kd, v, tril_mask)
        o_h = _hgrn_out(A, qg, v, [s.astype(BF16) for s in st], z_scr[rows, C_HG:C_HG + HG_WIDTH], gnorm)
        for h in range(HG_HEADS):
            sn_ref[b, h] = (st[h] * eGL[:, _head(h)] + U[h]).T
        mix_ref[rows, 0:HG_WIDTH] = o_h.astype(BF16)

        k_all = jnp.concatenate([kc_ref[b], z_scr[rows, C_AK:C_AK + LANES]], axis=0)
        v_all = jnp.concatenate([vc_ref[b], z_scr[rows, C_AV:C_AV + LANES]], axis=0)
        k2 = _dup_halves(k_all, low_k)
        v2 = _dup_halves(v_all, low_k)
        for g in range(ATTN_KV_HEADS):
            cq = C_AQ + g * ATTN_GROUP * ATTN_HEAD_DIM
            s = _swa_scores(z_scr[rows, cq:cq + LANES], z_scr[rows, cq + LANES:cq + 2 * LANES],
                            k2[g].astype(BF16), bias_ref[0, g], low)
            b0, b1 = _swa_out(*_swa_probs(s, sink_ref[g]), v2[g].astype(BF16), low)
            co = HG_WIDTH + g * ATTN_GROUP * ATTN_HEAD_DIM
            mix_ref[rows, co:co + LANES] = b0.astype(BF16)
            mix_ref[rows, co + LANES:co + 2 * LANES] = b1.astype(BF16)
        return carry

    lax.fori_loop(0, nb, batch_body, 0)


def _mixers_sample(x, w_in, lbl, gnorm, sinks, k_cache, v_cache, s0, layer):
    nb, ts, D = x.shape
    wlen = k_cache.shape[1]
    bias = _alibi_tables(ts, wlen + ts, wlen, False)
    sink = _sink_rows(sinks, ts)
    kern = functools.partial(_mixer_sample_kernel, layer, nb, ts)
    n = nb * ts
    return pl.pallas_call(
        kern,
        out_shape=[
            jax.ShapeDtypeStruct((n, D), BF16),
            jax.ShapeDtypeStruct((n, LANES), F32),
            jax.ShapeDtypeStruct((n, LANES), F32),
            jax.ShapeDtypeStruct((nb, HG_HEADS, HG_DK, HG_DV), F32),
        ],
        scratch_shapes=[pltpu.VMEM((n, IN_COLS), F32)],
        compiler_params=pltpu.CompilerParams(vmem_limit_bytes=VMEM_LIMIT),
        name="mixers_sample",
    )(x.reshape(n, D), w_in, lbl, gnorm, bias, sink,
      k_cache.reshape(nb, wlen, LANES), v_cache.reshape(nb, wlen, LANES), s0)


def _mem_kv_kernel(d, m_ref, w_ref, k_ref, v_ref, kb_ref, vb_ref):
    hd = d // MEM_HEADS
    r = _dot(m_ref[...].astype(BF16), w_ref[...])
    for h in range(MEM_HEADS):
        k_ref[:, h, :] = r[:, h * hd:(h + 1) * hd]
        v_ref[:, h, :] = r[:, d + h * hd:d + (h + 1) * hd]
    kb_ref[...] = r[:, :d].astype(BF16)
    vb_ref[...] = r[:, d:].astype(BF16)


def _mem_kv(mem, w_kv, tm):
    n, d = mem.shape
    hd = d // MEM_HEADS
    row = lambda i: (i, 0)
    row3 = lambda i: (i, 0, 0)
    return pl.pallas_call(
        functools.partial(_mem_kv_kernel, d),
        grid=(n // tm,),
        in_specs=[pl.BlockSpec((tm, d), row), _resident(w_kv.shape)],
        out_specs=[pl.BlockSpec((tm, MEM_HEADS, hd), row3)] * 2 + [pl.BlockSpec((tm, d), row)] * 2,
        out_shape=[jax.ShapeDtypeStruct((n, MEM_HEADS, hd), F32)] * 2 + [jax.ShapeDtypeStruct((n, d), BF16)] * 2,
        compiler_params=pltpu.CompilerParams(
            dimension_semantics=("arbitrary",), vmem_limit_bytes=VMEM_LIMIT),
        name="mem_kv",
    )(mem, w_kv)


def _post_kernel(alpha, nb, tb, n_sub, x_ref, mix_ref, mk_ref, mv_ref, w_out_ref, w_q_ref, w_o_ref,
                 w_fi_ref, w_fo_ref, g_ref, b_ref, y_ref, o_scr, act_scr):
    d = x_ref.shape[-1]
    n = nb * tb
    ns = n // n_sub
    dff = w_fo_ref.shape[0]
    hd = d // MEM_HEADS
    subs = range(n_sub)
    rows = [slice(s * ns, (s + 1) * ns) for s in subs]
    x_all = x_ref[...].reshape(n, d)
    mix_all = mix_ref[...].reshape(n, d)

    x1 = [_layer_norm(alpha * x_all[rows[s]] + _dot(mix_all[rows[s]], w_out_ref[...]),
                      g_ref[0:1, :], b_ref[0:1, :]) for s in subs]
    q = [_dot(x1[s].astype(BF16), w_q_ref[...]).astype(BF16) for s in subs]

    seg = tb if tb < ns else ns
    pieces = [(slice(r0, r0 + seg), r0 // tb) for r0 in range(0, n, seg)]
    for h in range(MEM_HEADS):
        cols = slice(h * hd, (h + 1) * hd)
        for r, i in pieces:
            qh = q[r.start // ns][r.start % ns:r.start % ns + seg, cols]
            s = _dot_nt(qh, mk_ref[i, :, cols].astype(BF16)) * (hd ** -0.5)
            m = jnp.max(s, axis=-1, keepdims=True)
            p = jnp.exp(s - m)
            den = jnp.sum(p, axis=-1, keepdims=True)
            o = _dot(p.astype(BF16), mv_ref[i, :, cols].astype(BF16)) / den
            o_scr[r, cols] = o.astype(BF16)
    x2 = [_layer_norm(alpha * x1[s] + _dot(o_scr[rows[s], :], w_o_ref[...]),
                      g_ref[1:2, :], b_ref[1:2, :]) for s in subs]

    x2b = [x2[s].astype(BF16) for s in subs]
    for j in range(dff // MXU_COLS):
        cg = slice(j * MXU_COLS, (j + 1) * MXU_COLS)
        cu = slice(dff + j * MXU_COLS, dff + (j + 1) * MXU_COLS)
        for s in subs:
            gte = _dot(x2b[s], w_fi_ref[:, cg])
            up = _dot(x2b[s], w_fi_ref[:, cu])
            act_scr[rows[s], cg] = (gte * _sigmoid(gte) * up).astype(BF16)
    x3 = [_layer_norm(alpha * x2[s] + _dot(act_scr[rows[s], :], w_fo_ref[...]),
                      g_ref[2:3, :], b_ref[2:3, :]) for s in subs]
    y_ref[...] = jnp.concatenate(x3, axis=0).reshape(nb, tb, d)


def _post_blocks(x, mix, mk, mv, w_out, w_q, w_o, w_fi, w_fo, ln_g, ln_b, alpha, nb, tb, n_sub):
    B, T, d = x.shape
    dff = w_fo.shape[0]
    m_tok = mk.shape[1]
    n = nb * tb
    tile = lambda b, t: (b, t, 0)
    per_b = lambda b, t: (b, 0, 0)
    kern = functools.partial(_post_kernel, alpha, nb, tb, n_sub)
    return pl.pallas_call(
        kern,
        grid=(B // nb, T // tb),
        in_specs=[
            pl.BlockSpec((nb, tb, d), tile),
            pl.BlockSpec((nb, tb, d), tile),
            pl.BlockSpec((nb, m_tok, d), per_b),
            pl.BlockSpec((nb, m_tok, d), per_b),
            _resident(w_out.shape), _resident(w_q.shape), _resident(w_o.shape),
            _resident(w_fi.shape), _resident(w_fo.shape),
            _resident(ln_g.shape), _resident(ln_b.shape),
        ],
        out_specs=pl.BlockSpec((nb, tb, d), tile),
        out_shape=jax.ShapeDtypeStruct((B, T, d), F32),
        scratch_shapes=[pltpu.VMEM((n, d), BF16), pltpu.VMEM((n, dff), BF16)],
        compiler_params=pltpu.CompilerParams(
            dimension_semantics=("arbitrary", "arbitrary"), vmem_limit_bytes=VMEM_LIMIT),
        name="post_blocks",
    )(x, mix, mk, mv, w_out, w_q, w_o, w_fi, w_fo, ln_g, ln_b)


TT_PROMPT = 512
TM_POST = 512
SUB_POST = 2
TM_MEM = 512


def kernel(x_prompt, x_sample, cache_swa_k, cache_swa_v, state_hgrn, cache_mem_k, cache_mem_v, mem_prompt, w_in, hgrn_lb_logits, hgrn_norm_g, attn_sinks, w_out, w_mem_q, w_mem_kv, w_mem_o, w_ffn_in, w_ffn_out, ln_g, ln_b):
    depth = w_in.shape[0]
    alpha = (2.0 * depth) ** 0.25
    B, T, D = x_prompt.shape
    nbs, ts, _ = x_sample.shape
    m_tok = mem_prompt.shape[1]
    lbl = hgrn_lb_logits.astype(F32)

    yp, ys = x_prompt, x_sample
    outs = [[] for _ in range(8)]
    for l in range(depth):
        w_in_l = w_in[l].astype(BF16)
        w_out_l = w_out[l].astype(BF16)
        w_q_l = w_mem_q[l].astype(BF16)
        w_kv_l = w_mem_kv[l].astype(BF16)
        w_o_l = w_mem_o[l].astype(BF16)
        w_fi_l = w_ffn_in[l].astype(BF16)
        w_fo_l = w_ffn_out[l].astype(BF16)
        gnorm = hgrn_norm_g[l].reshape(1, HG_DV).astype(F32)

        mix_p, k_p, v_p, s_p = _mixers_prompt(yp, w_in_l, lbl, gnorm, attn_sinks[l], l, TT_PROMPT)
        mk_p, mv_p, mk_b, mv_b = _mem_kv(mem_prompt.reshape(B * m_tok, D), w_kv_l, TM_MEM)
        yp = _post_blocks(yp, mix_p, mk_b.reshape(B, m_tok, D), mv_b.reshape(B, m_tok, D),
                          w_out_l, w_q_l, w_o_l, w_fi_l, w_fo_l, ln_g[l], ln_b[l], alpha, 1, TM_POST, SUB_POST)

        mix_s, k_s, v_s, s_s = _mixers_sample(ys, w_in_l, lbl, gnorm, attn_sinks[l],
                                              cache_swa_k[l], cache_swa_v[l], state_hgrn[l], l)
        ys = _post_blocks(ys, mix_s.reshape(nbs, ts, D), cache_mem_k[l].astype(BF16).reshape(nbs, m_tok, D),
                          cache_mem_v[l].astype(BF16).reshape(nbs, m_tok, D),
                          w_out_l, w_q_l, w_o_l, w_fi_l, w_fo_l, ln_g[l], ln_b[l], alpha, nbs, ts, 1)

        wlen = k_p.shape[1]
        new = [k_p.reshape(B, wlen, ATTN_KV_HEADS, ATTN_HEAD_DIM),
               v_p.reshape(B, wlen, ATTN_KV_HEADS, ATTN_HEAD_DIM),
               s_p,
               mk_p.reshape(B, m_tok, MEM_HEADS, D // MEM_HEADS),
               mv_p.reshape(B, m_tok, MEM_HEADS, D // MEM_HEADS),
               k_s.reshape(nbs, ts, ATTN_KV_HEADS, ATTN_HEAD_DIM),
               v_s.reshape(nbs, ts, ATTN_KV_HEADS, ATTN_HEAD_DIM),
               s_s]
        for acc, a in zip(outs, new):
            acc.append(a)
    return (yp, ys) + tuple(jnp.stack(a) for a in outs)
```

```python
import functools

import numpy as np
import jax
import jax.numpy as jnp
from jax import lax
from jax.experimental import pallas as pl
from jax.experimental.pallas import tpu as pltpu

F32 = jnp.float32
BF16 = jnp.bfloat16

CHUNK = 64
HG_HEADS = 4
HG_DK = 128
HG_DV = 128
HG_WIDTH = HG_HEADS * HG_DV
ATTN_HEADS = 8
ATTN_KV_HEADS = 2
ATTN_HEAD_DIM = 64
ATTN_GROUP = ATTN_HEADS // ATTN_KV_HEADS
ATTN_WIDTH = ATTN_HEADS * ATTN_HEAD_DIM
WINDOW = 128
BAND = WINDOW // CHUNK + 1
MEM_HEADS = 4
NEG = -1e30

C_HQ, C_HF, C_HI, C_HG = 0, 512, 1024, 1536
C_AQ = 2048
C_AK = C_AQ + ATTN_WIDTH
C_AV = C_AK + ATTN_KV_HEADS * ATTN_HEAD_DIM
IN_COLS = C_AV + ATTN_KV_HEADS * ATTN_HEAD_DIM

LANES = 128
MXU_COLS = 256
VMEM_LIMIT = 56 * 1024 * 1024


LOG2E = 1.4426950408889634


def _sigmoid(x):
    return 1.0 / (1.0 + jnp.exp2(x * (-LOG2E)))


def _dot(a, b):
    return jnp.dot(a, b, preferred_element_type=F32)


def _dot_nt(a, b):
    return lax.dot_general(a, b, (((1,), (1,)), ((), ())), preferred_element_type=F32)


def _dot_tn(a, b):
    return lax.dot_general(a, b, (((0,), (0,)), ((), ())), preferred_element_type=F32)


def _layer_norm(y, g, b):
    mu = jnp.mean(y, axis=-1, keepdims=True)
    d = y - mu
    var = jnp.mean(d * d, axis=-1, keepdims=True)
    return d * lax.rsqrt(var + 1e-5) * g + b


def _lower_bound(lbl, layer):
    n = lbl.shape[0]
    rows = [lbl[i:i + 1, :] for i in range(n)]
    m = functools.reduce(jnp.maximum, rows)
    e = [jnp.exp(r - m) for r in rows]
    tot = functools.reduce(jnp.add, e)
    return functools.reduce(jnp.add, e[:layer + 1]) / tot


def _tril_consts(L):
    r = lax.broadcasted_iota(jnp.int32, (L, L), 0)
    c = lax.broadcasted_iota(jnp.int32, (L, L), 1)
    mask = c <= r
    tril = jnp.where(mask, 1.0, 0.0).astype(BF16)
    return mask, jnp.concatenate([tril, tril, tril], axis=1)


def _hgrn_gates(hq, hf, hi, lb, tril3):
    L = hq.shape[0]
    f = lb + (1.0 - lb) * _sigmoid(hf)
    logf = jnp.log(f)
    p0 = logf.astype(BF16)
    r0 = logf - p0.astype(F32)
    p1 = r0.astype(BF16)
    p2 = (r0 - p1.astype(F32)).astype(BF16)
    G = _dot(tril3, jnp.concatenate([p0, p1, p2], axis=0))
    eGL = jnp.exp(G[L - 1:L, :])
    qg = hq * _sigmoid(hq) * jnp.exp(G)
    kg = (1.0 - f) * jnp.exp2(G * (-LOG2E))
    return qg.astype(BF16), kg.astype(BF16), (kg * eGL).astype(BF16), hi.astype(BF16), eGL


def _head(h):
    return slice(h * HG_DK, (h + 1) * HG_DK)


def _hgrn_intra(qg, kg, kd, v, tril_mask):
    A = [jnp.where(tril_mask, _dot_nt(qg[:, _head(h)], kg[:, _head(h)]), 0.0).astype(BF16)
         for h in range(HG_HEADS)]
    U = [_dot_tn(v[:, _head(h)], kd[:, _head(h)]) for h in range(HG_HEADS)]
    return A, U


def _hgrn_out(A, qg, v, st_bf, hg, gnorm):
    gate = hg * _sigmoid(hg)
    outs = []
    for h in range(HG_HEADS):
        o = _dot(A[h], v[:, _head(h)]) + _dot_nt(qg[:, _head(h)], st_bf[h])
        ms = jnp.mean(o * o, axis=-1, keepdims=True)
        outs.append(o * lax.rsqrt(ms + 1e-6) * gnorm * gate[:, _head(h)])
    return jnp.concatenate(outs, axis=1)


def _dup_halves(a, low):
    sw = pltpu.roll(a, ATTN_HEAD_DIM, axis=1)
    return jnp.where(low, a, sw), jnp.where(low, sw, a)


def _swa_scores(qb0, qb1, k2, bias, low):
    return _dot_nt(_stack_q(qb0, qb1, low), k2) * (ATTN_HEAD_DIM ** -0.5) + bias


def _swa_probs(s, sink):
    m = jnp.maximum(jnp.max(s, axis=-1, keepdims=True), sink)
    p = jnp.exp(s - m)
    return p.astype(BF16), jnp.sum(p, axis=-1, keepdims=True) + jnp.exp(sink - m)


def _swa_out(p, den, v2, low):
    Lq = p.shape[0] // ATTN_GROUP
    o = _dot(p, v2) / den
    return (jnp.where(low, o[0:Lq], o[Lq:2 * Lq]), jnp.where(low, o[2 * Lq:3 * Lq], o[3 * Lq:4 * Lq]))


def _stack_q(qb0, qb1, low):
    return jnp.concatenate([jnp.where(low, qb0, 0.0), jnp.where(low, 0.0, qb0),
                            jnp.where(low, qb1, 0.0), jnp.where(low, 0.0, qb1)], axis=0).astype(BF16)


def _swa_probs_t(k2, qm, bias_t, sink_t):
    s = _dot_nt(k2, qm) * (ATTN_HEAD_DIM ** -0.5 * LOG2E) + bias_t
    m = jnp.maximum(jnp.max(s, axis=0, keepdims=True), sink_t)
    e = jnp.exp2(s - m)
    den = jnp.sum(e, axis=0, keepdims=True) + jnp.exp2(sink_t - m)
    return (e * (1.0 / den)).astype(BF16)


def _swa_out_t(vt2, p_t, low):
    Lq = p_t.shape[1] // ATTN_GROUP
    o_t = _dot(vt2, p_t)
    a0 = o_t[:, 0:2 * Lq].T
    a1 = o_t[:, 2 * Lq:4 * Lq].T
    return jnp.where(low, a0[0:Lq], a0[Lq:2 * Lq]), jnp.where(low, a1[0:Lq], a1[Lq:2 * Lq])


def _mixer_prompt_kernel(layer, tt, nt, x0_ref, xa_ref, xb_ref, w_in_ref, lbl_ref, gnorm_ref, bias_ref,
                         sink_ref, mix_ref, kp_ref, vp_ref, sp_ref,
                         za_scr, zb_scr, xc_scr, kh_scr, vh_scr, k2_scr, vt_scr, st_scr):
    k = pl.program_id(0)
    t0 = lax.rem(2 * k, nt)

    @pl.when(k == 0)
    def _():
        za_scr[...] = _dot(x0_ref[0].astype(BF16), w_in_ref[...])

    @pl.when(t0 == 0)
    def _():
        st_scr[...] = jnp.zeros_like(st_scr)
        kh_scr[...] = jnp.zeros_like(kh_scr)
        vh_scr[...] = jnp.zeros_like(vh_scr)

    consts = (_lower_bound(lbl_ref[...], layer), gnorm_ref[...])
    scr = (kh_scr, vh_scr, k2_scr, vt_scr, st_scr)

    _mix_tile(tt, za_scr, t0, 0, consts, bias_ref, sink_ref, mix_ref, scr,
              _proj_pieces(xa_ref, xc_scr.at[0], w_in_ref, zb_scr))
    _mix_tile(tt, zb_scr, t0 + 1, tt, consts, bias_ref, sink_ref, mix_ref, scr,
              _proj_pieces(xb_ref, xc_scr.at[1], w_in_ref, za_scr))

    @pl.when(t0 + 1 == nt - 1)
    def _():
        w = min(WINDOW, tt)
        kp_ref[0] = zb_scr[tt - w:tt, C_AK:C_AK + LANES]
        vp_ref[0] = zb_scr[tt - w:tt, C_AV:C_AV + LANES]
        for h in range(HG_HEADS):
            sp_ref[0, h] = st_scr[h].T


def _proj_pieces(x_ref, xb_scr, w_in_ref, z_dst):
    half = x_ref.shape[1] // 2

    def piece(j, r):
        cols = slice(j * MXU_COLS, (j + 1) * MXU_COLS)
        rows = slice(r * half, (r + 1) * half)
        def run():
            if j == 0:
                xb_scr[rows, :] = x_ref[0, rows, :].astype(BF16)
            z_dst[rows, cols] = _dot(xb_scr[rows, :], w_in_ref[:, cols])
        return run
    return [piece(j, r) for j in range(IN_COLS // MXU_COLS) for r in range(2)]


def _mix_tile(tt, z_scr, t, row0, consts, bias_ref, sink_ref, mix_ref, scr, next_proj):
    lb, gnorm = consts
    kh_scr, vh_scr, k2_scr, vt_scr, st_scr = scr
    n_chunks = tt // CHUNK
    hist = (BAND - 1) * CHUNK

    k_tile = z_scr[:, C_AK:C_AK + LANES]
    low_k = lax.broadcasted_iota(jnp.int32, (hist + tt, LANES), 1) < ATTN_HEAD_DIM
    ka, kb = _dup_halves(jnp.concatenate([kh_scr[...], k_tile], axis=0), low_k)
    k2_scr[0] = ka.astype(BF16)
    k2_scr[1] = kb.astype(BF16)
    v_tile = z_scr[:, C_AV:C_AV + LANES]
    v_hist = vh_scr[...]
    v_shift = (jnp.concatenate([v_hist, v_tile], axis=0),
               jnp.concatenate([v_hist[CHUNK:], v_tile, v_tile[tt - CHUNK:]], axis=0))
    for par in range(2):
        vt = v_shift[par].T.astype(BF16)
        for g in range(ATTN_KV_HEADS):
            blk = vt[g * ATTN_HEAD_DIM:(g + 1) * ATTN_HEAD_DIM]
            vt_scr[par, g] = jnp.concatenate([blk, blk], axis=0)
    kh_scr[...] = k_tile[tt - hist:]
    vh_scr[...] = v_tile[tt - hist:]

    tril_mask, tril3 = _tril_consts(CHUNK)
    low = lax.broadcasted_iota(jnp.int32, (CHUNK, LANES), 1) < ATTN_HEAD_DIM
    chunks = range(n_chunks)
    groups = range(ATTN_KV_HEADS)
    rows = [slice(c * CHUNK, (c + 1) * CHUNK) for c in chunks]
    out_rows = [slice(row0 + c * CHUNK, row0 + (c + 1) * CHUNK) for c in chunks]
    keys = [slice(c * CHUNK, (c + BAND) * CHUNK) for c in chunks]
    z = lambda c, col: z_scr[rows[c], col:col + HG_WIDTH]

    pieces = list(next_proj)
    n_pieces, n_slots, slot = len(pieces), 4 * n_chunks, [0]

    def emit_proj():
        slot[0] += 1
        while pieces and (n_pieces - len(pieces)) * n_slots < slot[0] * n_pieces:
            pieces.pop(0)()

    gates = []
    for c in chunks:
        gates.append(_hgrn_gates(z(c, C_HQ), z(c, C_HF), z(c, C_HI), lb, tril3))
        emit_proj()

    def probs(c, g):
        tb = BAND - 1 if c >= BAND - 1 else jnp.minimum(t * n_chunks + c, BAND - 1)
        cq = C_AQ + g * ATTN_GROUP * ATTN_HEAD_DIM
        qm = _stack_q(z_scr[rows[c], cq:cq + LANES], z_scr[rows[c], cq + LANES:cq + 2 * LANES], low)
        return _swa_probs_t(k2_scr[g, keys[c], :], qm, bias_ref[tb, g], sink_ref[g] * LOG2E)

    p_all = []
    for c in chunks:
        p_all.append([probs(c, g) for g in groups])
        emit_proj()
    intra = [_hgrn_intra(*gates[c][:4], tril_mask) for c in chunks]

    st = [st_scr[h] for h in range(HG_HEADS)]
    for c in chunks:
        qg, _, _, v, eGL = gates[c]
        A, U = intra[c]
        o_h = _hgrn_out(A, qg, v, [s.astype(BF16) for s in st], z(c, C_HG), gnorm)
        st = [st[h] * eGL[:, _head(h)] + U[h] for h in range(HG_HEADS)]
        mix_ref[0, out_rows[c], 0:HG_WIDTH] = o_h.astype(BF16)
        emit_proj()
    for h in range(HG_HEADS):
        st_scr[h] = st[h]

    for c in chunks:
        par = c % 2
        win = slice((c - par) * CHUNK, (c - par + BAND) * CHUNK)
        for g in groups:
            b0, b1 = _swa_out_t(vt_scr[par, g, :, win], p_all[c][g], low)
            co = HG_WIDTH + g * ATTN_GROUP * ATTN_HEAD_DIM
            mix_ref[0, out_rows[c], co:co + LANES] = b0.astype(BF16)
            mix_ref[0, out_rows[c], co + LANES:co + 2 * LANES] = b1.astype(BF16)
        emit_proj()
    assert not pieces


def _alibi_tables(lq, lk, q0, with_chunk_mask, scale=1.0):
    slopes = 2.0 ** (-8.0 * np.arange(1, ATTN_HEADS + 1) / ATTN_HEADS)
    dist = np.abs((q0 + np.arange(lq))[:, None] - np.arange(lk)[None, :]).astype(np.float64)
    base = -slopes.reshape(ATTN_KV_HEADS, ATTN_GROUP, 1, 1) * dist * scale
    base = base.reshape(ATTN_KV_HEADS, ATTN_GROUP * lq, lk)
    if not with_chunk_mask:
        return jnp.asarray(base[None], F32)
    tabs = []
    for tb in range(BAND):
        valid = (np.arange(lk) // CHUNK) >= (BAND - 1 - tb)
        tabs.append(np.where(valid[None, None, :], base, NEG))
    return jnp.asarray(np.stack(tabs), F32)


def _sink_rows(sinks, lq):
    return jnp.repeat(sinks.astype(F32).reshape(ATTN_KV_HEADS, ATTN_GROUP), lq, axis=1)[..., None]


def _resident(shape):
    nd = len(shape)
    return pl.BlockSpec(shape, lambda *_: (0,) * nd, pipeline_mode=pl.Buffered(1))


def _mixers_prompt(x, w_in, lbl, gnorm, sinks, layer, tt):
    B, T, D = x.shape
    nt = T // tt
    hist = (BAND - 1) * CHUNK
    w = min(WINDOW, T)
    bias = jnp.swapaxes(_alibi_tables(CHUNK, BAND * CHUNK, hist, True, LOG2E), -1, -2)
    sink = jnp.swapaxes(_sink_rows(sinks, CHUNK), -1, -2)
    kern = functools.partial(_mixer_prompt_kernel, layer, tt, nt)
    assert nt % 2 == 0
    n_tiles = B * nt
    half = nt // 2

    def tile(j):
        j = jnp.minimum(j, n_tiles - 1)
        return (j // nt, j % nt, 0)

    return pl.pallas_call(
        kern,
        grid=(n_tiles // 2,),
        in_specs=[
            pl.BlockSpec((1, tt, D), lambda k: (0, 0, 0), pipeline_mode=pl.Buffered(1)),
            pl.BlockSpec((1, tt, D), lambda k: tile(2 * k + 1)),
            pl.BlockSpec((1, tt, D), lambda k: tile(2 * k + 2)),
            _resident(w_in.shape),
            _resident(lbl.shape),
            _resident(gnorm.shape),
            _resident(bias.shape),
            _resident(sink.shape),
        ],
        out_specs=[
            pl.BlockSpec((1, 2 * tt, D), lambda k: (k // half, k % half, 0)),
            pl.BlockSpec((1, w, LANES), lambda k: (k // half, 0, 0)),
            pl.BlockSpec((1, w, LANES), lambda k: (k // half, 0, 0)),
            pl.BlockSpec((1, HG_HEADS, HG_DK, HG_DV), lambda k: (k // half, 0, 0, 0)),
        ],
        out_shape=[
            jax.ShapeDtypeStruct((B, T, D), BF16),
            jax.ShapeDtypeStruct((B, w, LANES), F32),
            jax.ShapeDtypeStruct((B, w, LANES), F32),
            jax.ShapeDtypeStruct((B, HG_HEADS, HG_DK, HG_DV), F32),
        ],
        scratch_shapes=[
            pltpu.VMEM((tt, IN_COLS), F32),
            pltpu.VMEM((tt, IN_COLS), F32),
            pltpu.VMEM((2, tt, D), BF16),
            pltpu.VMEM((hist, LANES), F32),
            pltpu.VMEM((hist, LANES), F32),
            pltpu.VMEM((ATTN_KV_HEADS, hist + tt, LANES), BF16),
            pltpu.VMEM((2, ATTN_KV_HEADS, LANES, hist + tt), BF16),
            pltpu.VMEM((HG_HEADS, HG_DV, HG_DK), F32),
        ],
        compiler_params=pltpu.CompilerParams(
            dimension_semantics=("arbitrary",), vmem_limit_bytes=VMEM_LIMIT),
        name="mixers_prompt",
    )(x, x, x, w_in, lbl, gnorm, bias, sink)


def _mixer_sample_kernel(layer, nb, ts, x_ref, w_in_ref, lbl_ref, gnorm_ref, bias_ref, sink_ref,
                         kc_ref, vc_ref, s0_ref, mix_ref, kn_ref, vn_ref, sn_ref, z_scr):
    wlen = kc_ref.shape[1]
    z_scr[...] = _dot(x_ref[...].astype(BF16), w_in_ref[...])
    kn_ref[...] = z_scr[:, C_AK:C_AK + LANES]
    vn_ref[...] = z_scr[:, C_AV:C_AV + LANES]

    lb = _lower_bound(lbl_ref[...], layer)
    gnorm = gnorm_ref[...]
    tril_mask, tril3 = _tril_consts(ts)
    low = lax.broadcasted_iota(jnp.int32, (ts, LANES), 1) < ATTN_HEAD_DIM
    low_k = lax.broadcasted_iota(jnp.int32, (wlen + ts, LANES), 1) < ATTN_HEAD_DIM

    def batch_body(b, carry):
        r0 = pl.multiple_of(b * ts, ts)
        rows = pl.ds(r0, ts)
        st = [s0_ref[b, h].T for h in range(HG_HEADS)]
        qg, kg, kd, v, eGL = _hgrn_gates(z_scr[rows, C_HQ:C_HQ + HG_WIDTH], z_scr[rows, C_HF:C_HF + HG_WIDTH],
                                         z_scr[rows, C_HI:C_HI + HG_WIDTH], lb, tril3)
        A, U = _hgrn_intra(qg, kg, kd, v, tril_mask)
        o_h = _hgrn_out(A, qg, v, [s.astype(BF16) for s in st], z_scr[rows, C_HG:C_HG + HG_WIDTH], gnorm)
        for h in range(HG_HEADS):
            sn_ref[b, h] = (st[h] * eGL[:, _head(h)] + U[h]).T
        mix_ref[rows, 0:HG_WIDTH] = o_h.astype(BF16)

        k_all = jnp.concatenate([kc_ref[b], z_scr[rows, C_AK:C_AK + LANES]], axis=0)
        v_all = jnp.concatenate([vc_ref[b], z_scr[rows, C_AV:C_AV + LANES]], axis=0)
        k2 = _dup_halves(k_all, low_k)
        v2 = _dup_halves(v_all, low_k)
        for g in range(ATTN_KV_HEADS):
            cq = C_AQ + g * ATTN_GROUP * ATTN_HEAD_DIM
            s = _swa_scores(z_scr[rows, cq:cq + LANES], z_scr[rows, cq + LANES:cq + 2 * LANES],
                            k2[g].astype(BF16), bias_ref[0, g], low)
            b0, b1 = _swa_out(*_swa_probs(s, sink_ref[g]), v2[g].astype(BF16), low)
            co = HG_WIDTH + g * ATTN_GROUP * ATTN_HEAD_DIM
            mix_ref[rows, co:co + LANES] = b0.astype(BF16)
            mix_ref[rows, co + LANES:co + 2 * LANES] = b1.astype(BF16)
        return carry

    lax.fori_loop(0, nb, batch_body, 0)


def _mixers_sample(x, w_in, lbl, gnorm, sinks, k_cache, v_cache, s0, layer):
    nb, ts, D = x.shape
    wlen = k_cache.shape[1]
    bias = _alibi_tables(ts, wlen + ts, wlen, False)
    sink = _sink_rows(sinks, ts)
    kern = functools.partial(_mixer_sample_kernel, layer, nb, ts)
    n = nb * ts
    return pl.pallas_call(
        kern,
        out_shape=[
            jax.ShapeDtypeStruct((n, D), BF16),
            jax.ShapeDtypeStruct((n, LANES), F32),
            jax.ShapeDtypeStruct((n, LANES), F32),
            jax.ShapeDtypeStruct((nb, HG_HEADS, HG_DK, HG_DV), F32),
        ],
        scratch_shapes=[pltpu.VMEM((n, IN_COLS), F32)],
        compiler_params=pltpu.CompilerParams(vmem_limit_bytes=VMEM_LIMIT),
        name="mixers_sample",
    )(x.reshape(n, D), w_in, lbl, gnorm, bias, sink,
      k_cache.reshape(nb, wlen, LANES), v_cache.reshape(nb, wlen, LANES), s0)


def _mem_kv_kernel(d, m_ref, w_ref, k_ref, v_ref, kb_ref, vb_ref):
    hd = d // MEM_HEADS
    r = _dot(m_ref[...].astype(BF16), w_ref[...])
    for h in range(MEM_HEADS):
        k_ref[:, h, :] = r[:, h * hd:(h + 1) * hd]
        v_ref[:, h, :] = r[:, d + h * hd:d + (h + 1) * hd]
    kb_ref[...] = r[:, :d].astype(BF16)
    vb_ref[...] = r[:, d:].astype(BF16)


def _mem_kv(mem, w_kv, tm):
    n, d = mem.shape
    hd = d // MEM_HEADS
    row = lambda i: (i, 0)
    row3 = lambda i: (i, 0, 0)
    return pl.pallas_call(
        functools.partial(_mem_kv_kernel, d),
        grid=(n // tm,),
        in_specs=[pl.BlockSpec((tm, d), row), _resident(w_kv.shape)],
        out_specs=[pl.BlockSpec((tm, MEM_HEADS, hd), row3)] * 2 + [pl.BlockSpec((tm, d), row)] * 2,
        out_shape=[jax.ShapeDtypeStruct((n, MEM_HEADS, hd), F32)] * 2 + [jax.ShapeDtypeStruct((n, d), BF16)] * 2,
        compiler_params=pltpu.CompilerParams(
            dimension_semantics=("arbitrary",), vmem_limit_bytes=VMEM_LIMIT),
        name="mem_kv",
    )(mem, w_kv)


def _post_kernel(alpha, nb, tb, n_sub, x_ref, mix_ref, mk_ref, mv_ref, w_out_ref, w_q_ref, w_o_ref,
                 w_fi_ref, w_fo_ref, g_ref, b_ref, y_ref, o_scr, act_scr):
    d = x_ref.shape[-1]
    n = nb * tb
    ns = n // n_sub
    dff = w_fo_ref.shape[0]
    hd = d // MEM_HEADS
    subs = range(n_sub)
    rows = [slice(s * ns, (s + 1) * ns) for s in subs]
    x_all = x_ref[...].reshape(n, d)
    mix_all = mix_ref[...].reshape(n, d)

    x1 = [_layer_norm(alpha * x_all[rows[s]] + _dot(mix_all[rows[s]], w_out_ref[...]),
                      g_ref[0:1, :], b_ref[0:1, :]) for s in subs]
    q = [_dot(x1[s].astype(BF16), w_q_ref[...]).astype(BF16) for s in subs]

    seg = tb if tb < ns else ns
    pieces = [(slice(r0, r0 + seg), r0 // tb) for r0 in range(0, n, seg)]
    for h in range(MEM_HEADS):
        cols = slice(h * hd, (h + 1) * hd)
        for r, i in pieces:
            qh = q[r.start // ns][r.start % ns:r.start % ns + seg, cols]
            s = _dot_nt(qh, mk_ref[i, :, cols].astype(BF16)) * (hd ** -0.5)
            m = jnp.max(s, axis=-1, keepdims=True)
            p = jnp.exp(s - m)
            den = jnp.sum(p, axis=-1, keepdims=True)
            o = _dot(p.astype(BF16), mv_ref[i, :, cols].astype(BF16)) / den
            o_scr[r, cols] = o.astype(BF16)
    x2 = [_layer_norm(alpha * x1[s] + _dot(o_scr[rows[s], :], w_o_ref[...]),
                      g_ref[1:2, :], b_ref[1:2, :]) for s in subs]

    x2b = [x2[s].astype(BF16) for s in subs]
    for j in range(dff // MXU_COLS):
        cg = slice(j * MXU_COLS, (j + 1) * MXU_COLS)
        cu = slice(dff + j * MXU_COLS, dff + (j + 1) * MXU_COLS)
        for s in subs:
            gte = _dot(x2b[s], w_fi_ref[:, cg])
            up = _dot(x2b[s], w_fi_ref[:, cu])
            act_scr[rows[s], cg] = (gte * _sigmoid(gte) * up).astype(BF16)
    x3 = [_layer_norm(alpha * x2[s] + _dot(act_scr[rows[s], :], w_fo_ref[...]),
                      g_ref[2:3, :], b_ref[2:3, :]) for s in subs]
    y_ref[...] = jnp.concatenate(x3, axis=0).reshape(nb, tb, d)


def _post_blocks(x, mix, mk, mv, w_out, w_q, w_o, w_fi, w_fo, ln_g, ln_b, alpha, nb, tb, n_sub):
    B, T, d = x.shape
    dff = w_fo.shape[0]
    m_tok = mk.shape[1]
    n = nb * tb
    tile = lambda b, t: (b, t, 0)
    per_b = lambda b, t: (b, 0, 0)
    kern = functools.partial(_post_kernel, alpha, nb, tb, n_sub)
    return pl.pallas_call(
        kern,
        grid=(B // nb, T // tb),
        in_specs=[
            pl.BlockSpec((nb, tb, d), tile),
            pl.BlockSpec((nb, tb, d), tile),
            pl.BlockSpec((nb, m_tok, d), per_b),
            pl.BlockSpec((nb, m_tok, d), per_b),
            _resident(w_out.shape), _resident(w_q.shape), _resident(w_o.shape),
            _resident(w_fi.shape), _resident(w_fo.shape),
            _resident(ln_g.shape), _resident(ln_b.shape),
        ],
        out_specs=pl.BlockSpec((nb, tb, d), tile),
        out_shape=jax.ShapeDtypeStruct((B, T, d), F32),
        scratch_shapes=[pltpu.VMEM((n, d), BF16), pltpu.VMEM((n, dff), BF16)],
        compiler_params=pltpu.CompilerParams(
            dimension_semantics=("arbitrary", "arbitrary"), vmem_limit_bytes=VMEM_LIMIT),
        name="post_blocks",
    )(x, mix, mk, mv, w_out, w_q, w_o, w_fi, w_fo, ln_g, ln_b)


TT_PROMPT = 512
TM_POST = 512
SUB_POST = 2
TM_MEM = 512


def kernel(x_prompt, x_sample, cache_swa_k, cache_swa_v, state_hgrn, cache_mem_k, cache_mem_v, mem_prompt, w_in, hgrn_lb_logits, hgrn_norm_g, attn_sinks, w_out, w_mem_q, w_mem_kv, w_mem_o, w_ffn_in, w_ffn_out, ln_g, ln_b):
    depth = w_in.shape[0]
    alpha = (2.0 * depth) ** 0.25
    B, T, D = x_prompt.shape
    nbs, ts, _ = x_sample.shape
    m_tok = mem_prompt.shape[1]
    lbl = hgrn_lb_logits.astype(F32)

    yp, ys = x_prompt, x_sample
    outs = [[] for _ in range(8)]
    for l in range(depth):
        w_in_l = w_in[l].astype(BF16)
        w_out_l = w_out[l].astype(BF16)
        w_q_l = w_mem_q[l].astype(BF16)
        w_kv_l = w_mem_kv[l].astype(BF16)
        w_o_l = w_mem_o[l].astype(BF16)
        w_fi_l = w_ffn_in[l].astype(BF16)
        w_fo_l = w_ffn_out[l].astype(BF16)
        gnorm = hgrn_norm_g[l].reshape(1, HG_DV).astype(F32)

        mix_p, k_p, v_p, s_p = _mixers_prompt(yp, w_in_l, lbl, gnorm, attn_sinks[l], l, TT_PROMPT)
        mk_p, mv_p, mk_b, mv_b = _mem_kv(mem_prompt.reshape(B * m_tok, D), w_kv_l, TM_MEM)
        yp = _post_blocks(yp, mix_p, mk_b.reshape(B, m_tok, D), mv_b.reshape(B, m_tok, D),
                          w_out_l, w_q_l, w_o_l, w_fi_l, w_fo_l, ln_g[l], ln_b[l], alpha, 1, TM_POST, SUB_POST)

        mix_s, k_s, v_s, s_s = _mixers_sample(ys, w_in_l, lbl, gnorm, attn_sinks[l],
                                              cache_swa_k[l], cache_swa_v[l], state_hgrn[l], l)
        ys = _post_blocks(ys, mix_s.reshape(nbs, ts, D), cache_mem_k[l].astype(BF16).reshape(nbs, m_tok, D),
                          cache_mem_v[l].astype(BF16).reshape(nbs, m_tok, D),
                          w_out_l, w_q_l, w_o_l, w_fi_l, w_fo_l, ln_g[l], ln_b[l], alpha, nbs, ts, 1)

        wlen = k_p.shape[1]
        new = [k_p.reshape(B, wlen, ATTN_KV_HEADS, ATTN_HEAD_DIM),
               v_p.reshape(B, wlen, ATTN_KV_HEADS, ATTN_HEAD_DIM),
               s_p,
               mk_p.reshape(B, m_tok, MEM_HEADS, D // MEM_HEADS),
               mv_p.reshape(B, m_tok, MEM_HEADS, D // MEM_HEADS),
               k_s.reshape(nbs, ts, ATTN_KV_HEADS, ATTN_HEAD_DIM),
               v_s.reshape(nbs, ts, ATTN_KV_HEADS, ATTN_HEAD_DIM),
               s_s]
        for acc, a in zip(outs, new):
            acc.append(a)
    return (yp, ys) + tuple(jnp.stack(a) for a in outs)
```

```python
import functools

import numpy as np
import jax
import jax.numpy as jnp
from jax import lax
from jax.experimental import pallas as pl
from jax.experimental.pallas import tpu as pltpu

F32 = jnp.float32
BF16 = jnp.bfloat16

CHUNK = 64
HG_HEADS = 4
HG_DK = 128
HG_DV = 128
HG_WIDTH = HG_HEADS * HG_DV
ATTN_HEADS = 8
ATTN_KV_HEADS = 2
ATTN_HEAD_DIM = 64
ATTN_GROUP = ATTN_HEADS // ATTN_KV_HEADS
ATTN_WIDTH = ATTN_HEADS * ATTN_HEAD_DIM
WINDOW = 128
BAND = WINDOW // CHUNK + 1
MEM_HEADS = 4
NEG = -1e30

C_HQ, C_HF, C_HI, C_HG = 0, 512, 1024, 1536
C_AQ = 2048
C_AK = C_AQ + ATTN_WIDTH
C_AV = C_AK + ATTN_KV_HEADS * ATTN_HEAD_DIM
IN_COLS = C_AV + ATTN_KV_HEADS * ATTN_HEAD_DIM

LANES = 128
MXU_COLS = 256
VMEM_LIMIT = 60 * 1024 * 1024


LOG2E = 1.4426950408889634


def _sigmoid(x):
    return 1.0 / (1.0 + jnp.exp2(x * (-LOG2E)))


def _dot(a, b):
    return jnp.dot(a, b, preferred_element_type=F32)


def _dot_nt(a, b):
    return lax.dot_general(a, b, (((1,), (1,)), ((), ())), preferred_element_type=F32)


def _dot_tn(a, b):
    return lax.dot_general(a, b, (((0,), (0,)), ((), ())), preferred_element_type=F32)


def _layer_norm(y, g, b):
    mu = jnp.mean(y, axis=-1, keepdims=True)
    d = y - mu
    var = jnp.mean(d * d, axis=-1, keepdims=True)
    return d * lax.rsqrt(var + 1e-5) * g + b


def _lower_bound(lbl, layer):
    n = lbl.shape[0]
    rows = [lbl[i:i + 1, :] for i in range(n)]
    m = functools.reduce(jnp.maximum, rows)
    e = [jnp.exp(r - m) for r in rows]
    tot = functools.reduce(jnp.add, e)
    return functools.reduce(jnp.add, e[:layer + 1]) / tot


def _tril_consts(L):
    r = lax.broadcasted_iota(jnp.int32, (L, L), 0)
    c = lax.broadcasted_iota(jnp.int32, (L, L), 1)
    mask = c <= r
    tril = jnp.where(mask, 1.0, 0.0).astype(BF16)
    return mask, jnp.concatenate([tril, tril, tril], axis=1)


def _hgrn_gates(hq, hf, hi, lb, tril3):
    L = hq.shape[0]
    f = lb + (1.0 - lb) * _sigmoid(hf)
    logf = jnp.log(f)
    p0 = logf.astype(BF16)
    r0 = logf - p0.astype(F32)
    p1 = r0.astype(BF16)
    p2 = (r0 - p1.astype(F32)).astype(BF16)
    G = _dot(tril3, jnp.concatenate([p0, p1, p2], axis=0))
    eGL = jnp.exp(G[L - 1:L, :])
    qg = hq * _sigmoid(hq) * jnp.exp(G)
    kg = (1.0 - f) * jnp.exp2(G * (-LOG2E))
    return qg.astype(BF16), kg.astype(BF16), (kg * eGL).astype(BF16), hi.astype(BF16), eGL


def _head(h):
    return slice(h * HG_DK, (h + 1) * HG_DK)


def _hgrn_intra(qg, kg, kd, v, tril_mask):
    A = [jnp.where(tril_mask, _dot_nt(qg[:, _head(h)], kg[:, _head(h)]), 0.0).astype(BF16)
         for h in range(HG_HEADS)]
    U = [_dot_tn(v[:, _head(h)], kd[:, _head(h)]) for h in range(HG_HEADS)]
    return A, U


def _hgrn_out(A, qg, v, st_bf, hg, gnorm):
    gate = hg * _sigmoid(hg)
    outs = []
    for h in range(HG_HEADS):
        o = _dot(A[h], v[:, _head(h)]) + _dot_nt(qg[:, _head(h)], st_bf[h])
        ms = jnp.mean(o * o, axis=-1, keepdims=True)
        outs.append(o * lax.rsqrt(ms + 1e-6) * gnorm * gate[:, _head(h)])
    return jnp.concatenate(outs, axis=1)


def _dup_halves(a, low):
    sw = pltpu.roll(a, ATTN_HEAD_DIM, axis=1)
    return jnp.where(low, a, sw), jnp.where(low, sw, a)


def _swa_scores(qb0, qb1, k2, bias, low):
    return _dot_nt(_stack_q(qb0, qb1, low), k2) * (ATTN_HEAD_DIM ** -0.5) + bias


def _swa_probs(s, sink):
    m = jnp.maximum(jnp.max(s, axis=-1, keepdims=True), sink)
    p = jnp.exp(s - m)
    return p.astype(BF16), jnp.sum(p, axis=-1, keepdims=True) + jnp.exp(sink - m)


def _swa_out(p, den, v2, low):
    Lq = p.shape[0] // ATTN_GROUP
    o = _dot(p, v2) / den
    return (jnp.where(low, o[0:Lq], o[Lq:2 * Lq]), jnp.where(low, o[2 * Lq:3 * Lq], o[3 * Lq:4 * Lq]))


def _stack_q(qb0, qb1, low):
    return jnp.concatenate([jnp.where(low, qb0, 0.0), jnp.where(low, 0.0, qb0),
                            jnp.where(low, qb1, 0.0), jnp.where(low, 0.0, qb1)], axis=0).astype(BF16)


def _swa_probs_t(k2, qm, bias_t, sink_t):
    s = _dot_nt(k2, qm) * (ATTN_HEAD_DIM ** -0.5 * LOG2E) + bias_t
    m = jnp.maximum(jnp.max(s, axis=0, keepdims=True), sink_t)
    e = jnp.exp2(s - m)
    den = jnp.sum(e, axis=0, keepdims=True) + jnp.exp2(sink_t - m)
    return (e * (1.0 / den)).astype(BF16)


def _swa_out_t(vt2, p_t, low):
    Lq = p_t.shape[1] // ATTN_GROUP
    o_t = _dot(vt2, p_t)
    a0 = o_t[:, 0:2 * Lq].T
    a1 = o_t[:, 2 * Lq:4 * Lq].T
    return jnp.where(low, a0[0:Lq], a0[Lq:2 * Lq]), jnp.where(low, a1[0:Lq], a1[Lq:2 * Lq])


def _mixer_prompt_kernel(layer, tt, nt, x0_ref, xa_ref, xb_ref, w_in_ref, lbl_ref, gnorm_ref, bias_ref,
                         sink_ref, mix_ref, kp_ref, vp_ref, sp_ref,
                         za_scr, zb_scr, xc_scr, kh_scr, vh_scr, k2_scr, vt_scr, st_scr):
    k = pl.program_id(0)
    t0 = lax.rem(2 * k, nt)

    @pl.when(k == 0)
    def _():
        za_scr[...] = _dot(x0_ref[0].astype(BF16), w_in_ref[...])

    @pl.when(t0 == 0)
    def _():
        st_scr[...] = jnp.zeros_like(st_scr)
        kh_scr[...] = jnp.zeros_like(kh_scr)
        vh_scr[...] = jnp.zeros_like(vh_scr)

    consts = (_lower_bound(lbl_ref[...], layer), gnorm_ref[...])
    scr = (kh_scr, vh_scr, k2_scr, vt_scr, st_scr)

    _mix_tile(tt, za_scr, t0, 0, consts, bias_ref, sink_ref, mix_ref, scr,
              _proj_pieces(xa_ref, xc_scr.at[0], w_in_ref, zb_scr))
    _mix_tile(tt, zb_scr, t0 + 1, tt, consts, bias_ref, sink_ref, mix_ref, scr,
              _proj_pieces(xb_ref, xc_scr.at[1], w_in_ref, za_scr))

    @pl.when(t0 + 1 == nt - 1)
    def _():
        w = min(WINDOW, tt)
        kp_ref[0] = zb_scr[tt - w:tt, C_AK:C_AK + LANES]
        vp_ref[0] = zb_scr[tt - w:tt, C_AV:C_AV + LANES]
        for h in range(HG_HEADS):
            sp_ref[0, h] = st_scr[h].T


def _proj_pieces(x_ref, xb_scr, w_in_ref, z_dst):
    half = x_ref.shape[1] // 2

    def piece(j, r):
        cols = slice(j * MXU_COLS, (j + 1) * MXU_COLS)
        rows = slice(r * half, (r + 1) * half)
        def run():
            if j == 0:
                xb_scr[rows, :] = x_ref[0, rows, :].astype(BF16)
            z_dst[rows, cols] = _dot(xb_scr[rows, :], w_in_ref[:, cols])
        return run
    return [piece(j, r) for j in range(IN_COLS // MXU_COLS) for r in range(2)]


def _mix_tile(tt, z_scr, t, row0, consts, bias_ref, sink_ref, mix_ref, scr, next_proj):
    lb, gnorm = consts
    kh_scr, vh_scr, k2_scr, vt_scr, st_scr = scr
    n_chunks = tt // CHUNK
    hist = (BAND - 1) * CHUNK

    k_tile = z_scr[:, C_AK:C_AK + LANES]
    low_k = lax.broadcasted_iota(jnp.int32, (hist + tt, LANES), 1) < ATTN_HEAD_DIM
    ka, kb = _dup_halves(jnp.concatenate([kh_scr[...], k_tile], axis=0), low_k)
    k2_scr[0] = ka.astype(BF16)
    k2_scr[1] = kb.astype(BF16)
    v_tile = z_scr[:, C_AV:C_AV + LANES]
    v_hist = vh_scr[...]
    v_shift = (jnp.concatenate([v_hist, v_tile], axis=0),
               jnp.concatenate([v_hist[CHUNK:], v_tile, v_tile[tt - CHUNK:]], axis=0))
    for par in range(2):
        vt = v_shift[par].T.astype(BF16)
        for g in range(ATTN_KV_HEADS):
            blk = vt[g * ATTN_HEAD_DIM:(g + 1) * ATTN_HEAD_DIM]
            vt_scr[par, g] = jnp.concatenate([blk, blk], axis=0)
    kh_scr[...] = k_tile[tt - hist:]
    vh_scr[...] = v_tile[tt - hist:]

    tril_mask, tril3 = _tril_consts(CHUNK)
    low = lax.broadcasted_iota(jnp.int32, (CHUNK, LANES), 1) < ATTN_HEAD_DIM
    chunks = range(n_chunks)
    groups = range(ATTN_KV_HEADS)
    rows = [slice(c * CHUNK, (c + 1) * CHUNK) for c in chunks]
    out_rows = [slice(row0 + c * CHUNK, row0 + (c + 1) * CHUNK) for c in chunks]
    keys = [slice(c * CHUNK, (c + BAND) * CHUNK) for c in chunks]
    z = lambda c, col: z_scr[rows[c], col:col + HG_WIDTH]

    pieces = list(next_proj)
    n_pieces, n_slots, slot = len(pieces), 4 * n_chunks, [0]

    def emit_proj():
        slot[0] += 1
        while pieces and (n_pieces - len(pieces)) * n_slots < slot[0] * n_pieces:
            pieces.pop(0)()

    gates = []
    for c in chunks:
        gates.append(_hgrn_gates(z(c, C_HQ), z(c, C_HF), z(c, C_HI), lb, tril3))
        emit_proj()

    def probs(c, g):
        tb = BAND - 1 if c >= BAND - 1 else jnp.minimum(t * n_chunks + c, BAND - 1)
        cq = C_AQ + g * ATTN_GROUP * ATTN_HEAD_DIM
        qm = _stack_q(z_scr[rows[c], cq:cq + LANES], z_scr[rows[c], cq + LANES:cq + 2 * LANES], low)
        return _swa_probs_t(k2_scr[g, keys[c], :], qm, bias_ref[tb, g], sink_ref[g] * LOG2E)

    p_all = []
    for c in chunks:
        p_all.append([probs(c, g) for g in groups])
        emit_proj()
    intra = [_hgrn_intra(*gates[c][:4], tril_mask) for c in chunks]

    st = [st_scr[h] for h in range(HG_HEADS)]
    for c in chunks:
        qg, _, _, v, eGL = gates[c]
        A, U = intra[c]
        o_h = _hgrn_out(A, qg, v, [s.astype(BF16) for s in st], z(c, C_HG), gnorm)
        st = [st[h] * eGL[:, _head(h)] + U[h] for h in range(HG_HEADS)]
        mix_ref[0, out_rows[c], 0:HG_WIDTH] = o_h.astype(BF16)
        emit_proj()
    for h in range(HG_HEADS):
        st_scr[h] = st[h]

    for c in chunks:
        par = c % 2
        win = slice((c - par) * CHUNK, (c - par + BAND) * CHUNK)
        for g in groups:
            b0, b1 = _swa_out_t(vt_scr[par, g, :, win], p_all[c][g], low)
            co = HG_WIDTH + g * ATTN_GROUP * ATTN_HEAD_DIM
            mix_ref[0, out_rows[c], co:co + LANES] = b0.astype(BF16)
            mix_ref[0, out_rows[c], co + LANES:co + 2 * LANES] = b1.astype(BF16)
        emit_proj()
    assert not pieces


def _alibi_tables(lq, lk, q0, with_chunk_mask, scale=1.0):
    slopes = 2.0 ** (-8.0 * np.arange(1, ATTN_HEADS + 1) / ATTN_HEADS)
    dist = np.abs((q0 + np.arange(lq))[:, None] - np.arange(lk)[None, :]).astype(np.float64)
    base = -slopes.reshape(ATTN_KV_HEADS, ATTN_GROUP, 1, 1) * dist * scale
    base = base.reshape(ATTN_KV_HEADS, ATTN_GROUP * lq, lk)
    if not with_chunk_mask:
        return jnp.asarray(base[None], F32)
    tabs = []
    for tb in range(BAND):
        valid = (np.arange(lk) // CHUNK) >= (BAND - 1 - tb)
        tabs.append(np.where(valid[None, None, :], base, NEG))
    return jnp.asarray(np.stack(tabs), F32)


def _sink_rows(sinks, lq):
    return jnp.repeat(sinks.astype(F32).reshape(ATTN_KV_HEADS, ATTN_GROUP), lq, axis=1)[..., None]


def _resident(shape):
    nd = len(shape)
    return pl.BlockSpec(shape, lambda *_: (0,) * nd, pipeline_mode=pl.Buffered(1))


def _mixers_prompt(x, w_in, lbl, gnorm, sinks, layer, tt):
    B, T, D = x.shape
    nt = T // tt
    hist = (BAND - 1) * CHUNK
    w = min(WINDOW, T)
    bias = jnp.swapaxes(_alibi_tables(CHUNK, BAND * CHUNK, hist, True, LOG2E), -1, -2)
    sink = jnp.swapaxes(_sink_rows(sinks, CHUNK), -1, -2)
    kern = functools.partial(_mixer_prompt_kernel, layer, tt, nt)
    assert nt % 2 == 0
    n_tiles = B * nt
    half = nt // 2

    def tile(j):
        j = jnp.minimum(j, n_tiles - 1)
        return (j // nt, j % nt, 0)

    return pl.pallas_call(
        kern,
        grid=(n_tiles // 2,),
        in_specs=[
            pl.BlockSpec((1, tt, D), lambda k: (0, 0, 0), pipeline_mode=pl.Buffered(1)),
            pl.BlockSpec((1, tt, D), lambda k: tile(2 * k + 1)),
            pl.BlockSpec((1, tt, D), lambda k: tile(2 * k + 2)),
            _resident(w_in.shape),
            _resident(lbl.shape),
            _resident(gnorm.shape),
            _resident(bias.shape),
            _resident(sink.shape),
        ],
        out_specs=[
            pl.BlockSpec((1, 2 * tt, D), lambda k: (k // half, k % half, 0)),
            pl.BlockSpec((1, w, LANES), lambda k: (k // half, 0, 0)),
            pl.BlockSpec((1, w, LANES), lambda k: (k // half, 0, 0)),
            pl.BlockSpec((1, HG_HEADS, HG_DK, HG_DV), lambda k: (k // half, 0, 0, 0)),
        ],
        out_shape=[
            jax.ShapeDtypeStruct((B, T, D), BF16),
            jax.ShapeDtypeStruct((B, w, LANES), F32),
            jax.ShapeDtypeStruct((B, w, LANES), F32),
            jax.ShapeDtypeStruct((B, HG_HEADS, HG_DK, HG_DV), F32),
        ],
        scratch_shapes=[
            pltpu.VMEM((tt, IN_COLS), F32),
            pltpu.VMEM((tt, IN_COLS), F32),
            pltpu.VMEM((2, tt, D), BF16),
            pltpu.VMEM((hist, LANES), F32),
            pltpu.VMEM((hist, LANES), F32),
            pltpu.VMEM((ATTN_KV_HEADS, hist + tt, LANES), BF16),
            pltpu.VMEM((2, ATTN_KV_HEADS, LANES, hist + tt), BF16),
            pltpu.VMEM((HG_HEADS, HG_DV, HG_DK), F32),
        ],
        compiler_params=pltpu.CompilerParams(
            dimension_semantics=("arbitrary",), vmem_limit_bytes=VMEM_LIMIT),
        name="mixers_prompt",
    )(x, x, x, w_in, lbl, gnorm, bias, sink)


def _mixer_sample_kernel(layer, nb, ts, x_ref, w_in_ref, lbl_ref, gnorm_ref, bias_ref, sink_ref,
                         kc_ref, vc_ref, s0_ref, mix_ref, kn_ref, vn_ref, sn_ref, z_scr):
    wlen = kc_ref.shape[1]
    z_scr[...] = _dot(x_ref[...].astype(BF16), w_in_ref[...])
    kn_ref[...] = z_scr[:, C_AK:C_AK + LANES]
    vn_ref[...] = z_scr[:, C_AV:C_AV + LANES]

    lb = _lower_bound(lbl_ref[...], layer)
    gnorm = gnorm_ref[...]
    tril_mask, tril3 = _tril_consts(ts)
    low = lax.broadcasted_iota(jnp.int32, (ts, LANES), 1) < ATTN_HEAD_DIM
    low_k = lax.broadcasted_iota(jnp.int32, (wlen + ts, LANES), 1) < ATTN_HEAD_DIM

    def batch_body(b, carry):
        r0 = pl.multiple_of(b * ts, ts)
        rows = pl.ds(r0, ts)
        st = [s0_ref[b, h].T for h in range(HG_HEADS)]
        qg, kg, kd, v, eGL = _hgrn_gates(z_scr[rows, C_HQ:C_HQ + HG_WIDTH], z_scr[rows, C_HF:C_HF + HG_WIDTH],
                                         z_scr[rows, C_HI:C_HI + HG_WIDTH], lb, tril3)
        A, U = _hgrn_intra(qg, kg, kd, v, tril_mask)
        o_h = _hgrn_out(A, qg, v, [s.astype(BF16) for s in st], z_scr[rows, C_HG:C_HG + HG_WIDTH], gnorm)
        for h in range(HG_HEADS):
            sn_ref[b, h] = (st[h] * eGL[:, _head(h)] + U[h]).T
        mix_ref[rows, 0:HG_WIDTH] = o_h.astype(BF16)

        k_all = jnp.concatenate([kc_ref[b], z_scr[rows, C_AK:C_AK + LANES]], axis=0)
        v_all = jnp.concatenate([vc_ref[b], z_scr[rows, C_AV:C_AV + LANES]], axis=0)
        k2 = _dup_halves(k_all, low_k)
        v2 = _dup_halves(v_all, low_k)
        for g in range(ATTN_KV_HEADS):
            cq = C_AQ + g * ATTN_GROUP * ATTN_HEAD_DIM
            s = _swa_scores(z_scr[rows, cq:cq + LANES], z_scr[rows, cq + LANES:cq + 2 * LANES],
                            k2[g].astype(BF16), bias_ref[0, g], low)
            b0, b1 = _swa_out(*_swa_probs(s, sink_ref[g]), v2[g].astype(BF16), low)
            co = HG_WIDTH + g * ATTN_GROUP * ATTN_HEAD_DIM
            mix_ref[rows, co:co + LANES] = b0.astype(BF16)
            mix_ref[rows, co + LANES:co + 2 * LANES] = b1.astype(BF16)
        return carry

    lax.fori_loop(0, nb, batch_body, 0)


def _mixers_sample(x, w_in, lbl, gnorm, sinks, k_cache, v_cache, s0, layer):
    nb, ts, D = x.shape
    wlen = k_cache.shape[1]
    bias = _alibi_tables(ts, wlen + ts, wlen, False)
    sink = _sink_rows(sinks, ts)
    kern = functools.partial(_mixer_sample_kernel, layer, nb, ts)
    n = nb * ts
    return pl.pallas_call(
        kern,
        out_shape=[
            jax.ShapeDtypeStruct((n, D), BF16),
            jax.ShapeDtypeStruct((n, LANES), F32),
            jax.ShapeDtypeStruct((n, LANES), F32),
            jax.ShapeDtypeStruct((nb, HG_HEADS, HG_DK, HG_DV), F32),
        ],
        scratch_shapes=[pltpu.VMEM((n, IN_COLS), F32)],
        compiler_params=pltpu.CompilerParams(vmem_limit_bytes=VMEM_LIMIT),
        name="mixers_sample",
    )(x.reshape(n, D), w_in, lbl, gnorm, bias, sink,
      k_cache.reshape(nb, wlen, LANES), v_cache.reshape(nb, wlen, LANES), s0)


def _mem_kv_kernel(d, m_ref, w_ref, k_ref, v_ref, kb_ref, vb_ref):
    hd = d // MEM_HEADS
    r = _dot(m_ref[...].astype(BF16), w_ref[...])
    for h in range(MEM_HEADS):
        k_ref[:, h, :] = r[:, h * hd:(h + 1) * hd]
        v_ref[:, h, :] = r[:, d + h * hd:d + (h + 1) * hd]
    kb_ref[...] = r[:, :d].astype(BF16)
    vb_ref[...] = r[:, d:].astype(BF16)


def _mem_kv(mem, w_kv, tm):
    n, d = mem.shape
    hd = d // MEM_HEADS
    row = lambda i: (i, 0)
    row3 = lambda i: (i, 0, 0)
    return pl.pallas_call(
        functools.partial(_mem_kv_kernel, d),
        grid=(n // tm,),
        in_specs=[pl.BlockSpec((tm, d), row), _resident(w_kv.shape)],
        out_specs=[pl.BlockSpec((tm, MEM_HEADS, hd), row3)] * 2 + [pl.BlockSpec((tm, d), row)] * 2,
        out_shape=[jax.ShapeDtypeStruct((n, MEM_HEADS, hd), F32)] * 2 + [jax.ShapeDtypeStruct((n, d), BF16)] * 2,
        compiler_params=pltpu.CompilerParams(
            dimension_semantics=("arbitrary",), vmem_limit_bytes=VMEM_LIMIT),
        name="mem_kv",
    )(mem, w_kv)


def _post_kernel(alpha, nb, tb, n_sub, x_ref, mix_ref, mk_ref, mv_ref, w_out_ref, w_q_ref, w_o_ref,
                 w_fi_ref, w_fo_ref, g_ref, b_ref, y_ref, o_scr, act_scr):
    d = x_ref.shape[-1]
    n = nb * tb
    ns = n // n_sub
    dff = w_fo_ref.shape[0]
    hd = d // MEM_HEADS
    subs = range(n_sub)
    rows = [slice(s * ns, (s + 1) * ns) for s in subs]
    x_all = x_ref[...].reshape(n, d)
    mix_all = mix_ref[...].reshape(n, d)

    x1 = [_layer_norm(alpha * x_all[rows[s]] + _dot(mix_all[rows[s]], w_out_ref[...]),
                      g_ref[0:1, :], b_ref[0:1, :]) for s in subs]
    q = [_dot(x1[s].astype(BF16), w_q_ref[...]).astype(BF16) for s in subs]

    seg = tb if tb < ns else ns
    pieces = [(slice(r0, r0 + seg), r0 // tb) for r0 in range(0, n, seg)]
    for h in range(MEM_HEADS):
        cols = slice(h * hd, (h + 1) * hd)
        for r, i in pieces:
            qh = q[r.start // ns][r.start % ns:r.start % ns + seg, cols]
            s = _dot_nt(qh, mk_ref[i, :, cols].astype(BF16)) * (hd ** -0.5)
            m = jnp.max(s, axis=-1, keepdims=True)
            p = jnp.exp(s - m)
            den = jnp.sum(p, axis=-1, keepdims=True)
            o = _dot(p.astype(BF16), mv_ref[i, :, cols].astype(BF16)) / den
            o_scr[r, cols] = o.astype(BF16)
    x2 = [_layer_norm(alpha * x1[s] + _dot(o_scr[rows[s], :], w_o_ref[...]),
                      g_ref[1:2, :], b_ref[1:2, :]) for s in subs]

    x2b = [x2[s].astype(BF16) for s in subs]
    for j in range(dff // MXU_COLS):
        cg = slice(j * MXU_COLS, (j + 1) * MXU_COLS)
        cu = slice(dff + j * MXU_COLS, dff + (j + 1) * MXU_COLS)
        for s in subs:
            gte = _dot(x2b[s], w_fi_ref[:, cg])
            up = _dot(x2b[s], w_fi_ref[:, cu])
            act_scr[rows[s], cg] = (gte * _sigmoid(gte) * up).astype(BF16)
    x3 = [_layer_norm(alpha * x2[s] + _dot(act_scr[rows[s], :], w_fo_ref[...]),
                      g_ref[2:3, :], b_ref[2:3, :]) for s in subs]
    y_ref[...] = jnp.concatenate(x3, axis=0).reshape(nb, tb, d)


def _post_blocks(x, mix, mk, mv, w_out, w_q, w_o, w_fi, w_fo, ln_g, ln_b, alpha, nb, tb, n_sub):
    B, T, d = x.shape
    dff = w_fo.shape[0]
    m_tok = mk.shape[1]
    n = nb * tb
    tile = lambda b, t: (b, t, 0)
    per_b = lambda b, t: (b, 0, 0)
    kern = functools.partial(_post_kernel, alpha, nb, tb, n_sub)
    return pl.pallas_call(
        kern,
        grid=(B // nb, T // tb),
        in_specs=[
            pl.BlockSpec((nb, tb, d), tile),
            pl.BlockSpec((nb, tb, d), tile),
            pl.BlockSpec((nb, m_tok, d), per_b),
            pl.BlockSpec((nb, m_tok, d), per_b),
            _resident(w_out.shape), _resident(w_q.shape), _resident(w_o.shape),
            _resident(w_fi.shape), _resident(w_fo.shape),
            _resident(ln_g.shape), _resident(ln_b.shape),
        ],
        out_specs=pl.BlockSpec((nb, tb, d), tile),
        out_shape=jax.ShapeDtypeStruct((B, T, d), F32),
        scratch_shapes=[pltpu.VMEM((n, d), BF16), pltpu.VMEM((n, dff), BF16)],
        compiler_params=pltpu.CompilerParams(
            dimension_semantics=("arbitrary", "arbitrary"), vmem_limit_bytes=VMEM_LIMIT),
        name="post_blocks",
    )(x, mix, mk, mv, w_out, w_q, w_o, w_fi, w_fo, ln_g, ln_b)


TT_PROMPT = 512
TM_POST = 1024
SUB_POST = 4
TM_MEM = 512


def kernel(x_prompt, x_sample, cache_swa_k, cache_swa_v, state_hgrn, cache_mem_k, cache_mem_v, mem_prompt, w_in, hgrn_lb_logits, hgrn_norm_g, attn_sinks, w_out, w_mem_q, w_mem_kv, w_mem_o, w_ffn_in, w_ffn_out, ln_g, ln_b):
    depth = w_in.shape[0]
    alpha = (2.0 * depth) ** 0.25
    B, T, D = x_prompt.shape
    nbs, ts, _ = x_sample.shape
    m_tok = mem_prompt.shape[1]
    lbl = hgrn_lb_logits.astype(F32)

    yp, ys = x_prompt, x_sample
    outs = [[] for _ in range(8)]
    for l in range(depth):
        w_in_l = w_in[l].astype(BF16)
        w_out_l = w_out[l].astype(BF16)
        w_q_l = w_mem_q[l].astype(BF16)
        w_kv_l = w_mem_kv[l].astype(BF16)
        w_o_l = w_mem_o[l].astype(BF16)
        w_fi_l = w_ffn_in[l].astype(BF16)
        w_fo_l = w_ffn_out[l].astype(BF16)
        gnorm = hgrn_norm_g[l].reshape(1, HG_DV).astype(F32)

        mix_p, k_p, v_p, s_p = _mixers_prompt(yp, w_in_l, lbl, gnorm, attn_sinks[l], l, TT_PROMPT)
        mk_p, mv_p, mk_b, mv_b = _mem_kv(mem_prompt.reshape(B * m_tok, D), w_kv_l, TM_MEM)
        yp = _post_blocks(yp, mix_p, mk_b.reshape(B, m_tok, D), mv_b.reshape(B, m_tok, D),
                          w_out_l, w_q_l, w_o_l, w_fi_l, w_fo_l, ln_g[l], ln_b[l], alpha, 1, TM_POST, SUB_POST)

        mix_s, k_s, v_s, s_s = _mixers_sample(ys, w_in_l, lbl, gnorm, attn_sinks[l],
                                              cache_swa_k[l], cache_swa_v[l], state_hgrn[l], l)
        ys = _post_blocks(ys, mix_s.reshape(nbs, ts, D), cache_mem_k[l].astype(BF16).reshape(nbs, m_tok, D),
                          cache_mem_v[l].astype(BF16).reshape(nbs, m_tok, D),
                          w_out_l, w_q_l, w_o_l, w_fi_l, w_fo_l, ln_g[l], ln_b[l], alpha, nbs, ts, 1)

        wlen = k_p.shape[1]
        new = [k_p.reshape(B, wlen, ATTN_KV_HEADS, ATTN_HEAD_DIM),
               v_p.reshape(B, wlen, ATTN_KV_HEADS, ATTN_HEAD_DIM),
               s_p,
               mk_p.reshape(B, m_tok, MEM_HEADS, D // MEM_HEADS),
               mv_p.reshape(B, m_tok, MEM_HEADS, D // MEM_HEADS),
               k_s.reshape(nbs, ts, ATTN_KV_HEADS, ATTN_HEAD_DIM),
               v_s.reshape(nbs, ts, ATTN_KV_HEADS, ATTN_HEAD_DIM),
               s_s]
        for acc, a in zip(outs, new):
            acc.append(a)
    return (yp, ys) + tuple(jnp.stack(a) for a in outs)
```

```python
import functools

import numpy as np
import jax
import jax.numpy as jnp
from jax import lax
from jax.experimental import pallas as pl
from jax.experimental.pallas import tpu as pltpu

F32 = jnp.float32
BF16 = jnp.bfloat16

CHUNK = 64
HG_HEADS = 4
HG_DK = 128
HG_DV = 128
HG_WIDTH = HG_HEADS * HG_DV
ATTN_HEADS = 8
ATTN_KV_HEADS = 2
ATTN_HEAD_DIM = 64
ATTN_GROUP = ATTN_HEADS // ATTN_KV_HEADS
ATTN_WIDTH = ATTN_HEADS * ATTN_HEAD_DIM
WINDOW = 128
BAND = WINDOW // CHUNK + 1
MEM_HEADS = 4
NEG = -1e30

C_HQ = 0
C_HF = C_HQ + HG_HEADS * HG_DK
C_HI = C_HF + HG_HEADS * HG_DK
C_HG = C_HI + HG_WIDTH
C_AQ = C_HG + HG_WIDTH
C_AK = C_AQ + ATTN_WIDTH
C_AV = C_AK + ATTN_KV_HEADS * ATTN_HEAD_DIM
IN_COLS = C_AV + ATTN_KV_HEADS * ATTN_HEAD_DIM

LANES = 128
MXU_COLS = 256
MIB = 1024 * 1024
VMEM_LIMIT_POST = 60 * MIB
VMEM_LIMIT = 48 * MIB


LOG2E = 1.4426950408889634


def _sigmoid(x):
    return 1.0 / (1.0 + jnp.exp2(x * (-LOG2E)))


def _dot(a, b):
    return jnp.dot(a, b, preferred_element_type=F32)


def _dot_nt(a, b):
    return lax.dot_general(a, b, (((1,), (1,)), ((), ())), preferred_element_type=F32)


def _dot_tn(a, b):
    return lax.dot_general(a, b, (((0,), (0,)), ((), ())), preferred_element_type=F32)


def _layer_norm(y, g, b):
    mu = jnp.mean(y, axis=-1, keepdims=True)
    d = y - mu
    var = jnp.mean(d * d, axis=-1, keepdims=True)
    return d * lax.rsqrt(var + 1e-5) * g + b


def _lower_bound(lbl, layer):
    n = lbl.shape[0]
    rows = [lbl[i:i + 1, :] for i in range(n)]
    m = functools.reduce(jnp.maximum, rows)
    e = [jnp.exp(r - m) for r in rows]
    tot = functools.reduce(jnp.add, e)
    return functools.reduce(jnp.add, e[:layer + 1]) / tot


def _tril_consts(L):
    r = lax.broadcasted_iota(jnp.int32, (L, L), 0)
    c = lax.broadcasted_iota(jnp.int32, (L, L), 1)
    mask = c <= r
    tril = jnp.where(mask, 1.0, 0.0).astype(BF16)
    return mask, jnp.concatenate([tril, tril, tril], axis=1)


def _hgrn_gates(hq, hf, hi, lb, tril3):
    L = hq.shape[0]
    f = lb + (1.0 - lb) * _sigmoid(hf)
    logf = jnp.log(f)
    p0 = logf.astype(BF16)
    r0 = logf - p0.astype(F32)
    p1 = r0.astype(BF16)
    p2 = (r0 - p1.astype(F32)).astype(BF16)
    G = _dot(tril3, jnp.concatenate([p0, p1, p2], axis=0))
    eGL = jnp.exp(G[L - 1:L, :])
    qg = hq * _sigmoid(hq) * jnp.exp(G)
    kg = (1.0 - f) * jnp.exp2(G * (-LOG2E))
    return qg.astype(BF16), kg.astype(BF16), (kg * eGL).astype(BF16), hi.astype(BF16), eGL


def _head(h):
    return slice(h * HG_DK, (h + 1) * HG_DK)


def _hgrn_intra(qg, kg, kd, v, tril_mask):
    A = [jnp.where(tril_mask, _dot_nt(qg[:, _head(h)], kg[:, _head(h)]), 0.0).astype(BF16)
         for h in range(HG_HEADS)]
    U = [_dot_tn(v[:, _head(h)], kd[:, _head(h)]) for h in range(HG_HEADS)]
    return A, U


def _hgrn_out(A, qg, v, st_bf, hg, gnorm):
    gate = hg * _sigmoid(hg)
    outs = []
    for h in range(HG_HEADS):
        o = _dot(A[h], v[:, _head(h)]) + _dot_nt(qg[:, _head(h)], st_bf[h])
        ms = jnp.mean(o * o, axis=-1, keepdims=True)
        outs.append(o * lax.rsqrt(ms + 1e-6) * gnorm * gate[:, _head(h)])
    return jnp.concatenate(outs, axis=1)


def _dup_halves(a, low):
    sw = pltpu.roll(a, ATTN_HEAD_DIM, axis=1)
    return jnp.where(low, a, sw), jnp.where(low, sw, a)


def _swa_scores(qb0, qb1, k2, bias, low):
    return _dot_nt(_stack_q(qb0, qb1, low), k2) * (ATTN_HEAD_DIM ** -0.5) + bias


def _swa_probs(s, sink):
    m = jnp.maximum(jnp.max(s, axis=-1, keepdims=True), sink)
    p = jnp.exp(s - m)
    return p.astype(BF16), jnp.sum(p, axis=-1, keepdims=True) + jnp.exp(sink - m)


def _swa_out(p, den, v2, low):
    Lq = p.shape[0] // ATTN_GROUP
    o = _dot(p, v2) / den
    return (jnp.where(low, o[0:Lq], o[Lq:2 * Lq]), jnp.where(low, o[2 * Lq:3 * Lq], o[3 * Lq:4 * Lq]))


def _stack_q(qb0, qb1, low):
    return jnp.concatenate([jnp.where(low, qb0, 0.0), jnp.where(low, 0.0, qb0),
                            jnp.where(low, qb1, 0.0), jnp.where(low, 0.0, qb1)], axis=0).astype(BF16)


def _swa_probs_t(k2, qm, bias_t, sink_t):
    s = _dot_nt(k2, qm) * (ATTN_HEAD_DIM ** -0.5 * LOG2E) + bias_t
    m = jnp.maximum(jnp.max(s, axis=0, keepdims=True), sink_t)
    e = jnp.exp2(s - m)
    den = jnp.sum(e, axis=0, keepdims=True) + jnp.exp2(sink_t - m)
    return (e * (1.0 / den)).astype(BF16)


def _swa_out_t(vt2, p_t, low):
    Lq = p_t.shape[1] // ATTN_GROUP
    o_t = _dot(vt2, p_t)
    a0 = o_t[:, 0:2 * Lq].T
    a1 = o_t[:, 2 * Lq:4 * Lq].T
    return jnp.where(low, a0[0:Lq], a0[Lq:2 * Lq]), jnp.where(low, a1[0:Lq], a1[Lq:2 * Lq])


def _mixer_prompt_kernel(layer, tt, nt, x0_ref, xa_ref, xb_ref, w_in_ref, lbl_ref, gnorm_ref, bias_ref,
                         sink_ref, mix_ref, kp_ref, vp_ref, sp_ref,
                         za_scr, zb_scr, xc_scr, kh_scr, vh_scr, k2_scr, vt_scr, st_scr):
    k = pl.program_id(0)
    t0 = lax.rem(2 * k, nt)

    @pl.when(k == 0)
    def _():
        za_scr[...] = _dot(x0_ref[0].astype(BF16), w_in_ref[...])

    @pl.when(t0 == 0)
    def _():
        st_scr[...] = jnp.zeros_like(st_scr)
        kh_scr[...] = jnp.zeros_like(kh_scr)
        vh_scr[...] = jnp.zeros_like(vh_scr)

    consts = (_lower_bound(lbl_ref[...], layer), gnorm_ref[...])
    scr = (kh_scr, vh_scr, k2_scr, vt_scr, st_scr)

    _mix_tile(tt, za_scr, t0, 0, consts, bias_ref, sink_ref, mix_ref, scr,
              _proj_pieces(xa_ref, xc_scr.at[0], w_in_ref, zb_scr))
    _mix_tile(tt, zb_scr, t0 + 1, tt, consts, bias_ref, sink_ref, mix_ref, scr,
              _proj_pieces(xb_ref, xc_scr.at[1], w_in_ref, za_scr))

    @pl.when(t0 + 1 == nt - 1)
    def _():
        w = min(WINDOW, tt)
        kp_ref[0] = zb_scr[tt - w:tt, C_AK:C_AK + LANES]
        vp_ref[0] = zb_scr[tt - w:tt, C_AV:C_AV + LANES]
        for h in range(HG_HEADS):
            sp_ref[0, h] = st_scr[h].T


def _proj_pieces(x_ref, xb_scr, w_in_ref, z_dst):
    half = x_ref.shape[1] // 2

    def piece(j, r):
        cols = slice(j * MXU_COLS, (j + 1) * MXU_COLS)
        rows = slice(r * half, (r + 1) * half)
        def run():
            if j == 0:
                xb_scr[rows, :] = x_ref[0, rows, :].astype(BF16)
            z_dst[rows, cols] = _dot(xb_scr[rows, :], w_in_ref[:, cols])
        return run
    return [piece(j, r) for j in range(IN_COLS // MXU_COLS) for r in range(2)]


def _mix_tile(tt, z_scr, t, row0, consts, bias_ref, sink_ref, mix_ref, scr, next_proj):
    lb, gnorm = consts
    kh_scr, vh_scr, k2_scr, vt_scr, st_scr = scr
    n_chunks = tt // CHUNK
    hist = (BAND - 1) * CHUNK

    k_tile = z_scr[:, C_AK:C_AK + LANES]
    low_k = lax.broadcasted_iota(jnp.int32, (hist + tt, LANES), 1) < ATTN_HEAD_DIM
    ka, kb = _dup_halves(jnp.concatenate([kh_scr[...], k_tile], axis=0), low_k)
    k2_scr[0] = ka.astype(BF16)
    k2_scr[1] = kb.astype(BF16)
    v_tile = z_scr[:, C_AV:C_AV + LANES]
    v_hist = vh_scr[...]
    v_shift = (jnp.concatenate([v_hist, v_tile], axis=0),
               jnp.concatenate([v_hist[CHUNK:], v_tile, v_tile[tt - CHUNK:]], axis=0))
    for par in range(2):
        vt = v_shift[par].T.astype(BF16)
        for g in range(ATTN_KV_HEADS):
            blk = vt[g * ATTN_HEAD_DIM:(g + 1) * ATTN_HEAD_DIM]
            vt_scr[par, g] = jnp.concatenate([blk, blk], axis=0)
    kh_scr[...] = k_tile[tt - hist:]
    vh_scr[...] = v_tile[tt - hist:]

    tril_mask, tril3 = _tril_consts(CHUNK)
    low = lax.broadcasted_iota(jnp.int32, (CHUNK, LANES), 1) < ATTN_HEAD_DIM
    chunks = range(n_chunks)
    groups = range(ATTN_KV_HEADS)
    rows = [slice(c * CHUNK, (c + 1) * CHUNK) for c in chunks]
    out_rows = [slice(row0 + c * CHUNK, row0 + (c + 1) * CHUNK) for c in chunks]
    keys = [slice(c * CHUNK, (c + BAND) * CHUNK) for c in chunks]
    z = lambda c, col: z_scr[rows[c], col:col + HG_WIDTH]

    pieces = list(next_proj)
    n_pieces, n_slots, slot = len(pieces), 4 * n_chunks, [0]

    def emit_proj():
        slot[0] += 1
        while pieces and (n_pieces - len(pieces)) * n_slots < slot[0] * n_pieces:
            pieces.pop(0)()

    gates = []
    for c in chunks:
        gates.append(_hgrn_gates(z(c, C_HQ), z(c, C_HF), z(c, C_HI), lb, tril3))
        emit_proj()

    def probs(c, g):
        tb = BAND - 1 if c >= BAND - 1 else jnp.minimum(t * n_chunks + c, BAND - 1)
        cq = C_AQ + g * ATTN_GROUP * ATTN_HEAD_DIM
        qm = _stack_q(z_scr[rows[c], cq:cq + LANES], z_scr[rows[c], cq + LANES:cq + 2 * LANES], low)
        return _swa_probs_t(k2_scr[g, keys[c], :], qm, bias_ref[tb, g], sink_ref[g] * LOG2E)

    p_all = []
    for c in chunks:
        p_all.append([probs(c, g) for g in groups])
        emit_proj()
    intra = [_hgrn_intra(*gates[c][:4], tril_mask) for c in chunks]

    st = [st_scr[h] for h in range(HG_HEADS)]
    for c in chunks:
        qg, _, _, v, eGL = gates[c]
        A, U = intra[c]
        o_h = _hgrn_out(A, qg, v, [s.astype(BF16) for s in st], z(c, C_HG), gnorm)
        st = [st[h] * eGL[:, _head(h)] + U[h] for h in range(HG_HEADS)]
        mix_ref[0, out_rows[c], 0:HG_WIDTH] = o_h.astype(BF16)
        emit_proj()
    for h in range(HG_HEADS):
        st_scr[h] = st[h]

    for c in chunks:
        par = c % 2
        win = slice((c - par) * CHUNK, (c - par + BAND) * CHUNK)
        for g in groups:
            b0, b1 = _swa_out_t(vt_scr[par, g, :, win], p_all[c][g], low)
            co = HG_WIDTH + g * ATTN_GROUP * ATTN_HEAD_DIM
            mix_ref[0, out_rows[c], co:co + LANES] = b0.astype(BF16)
            mix_ref[0, out_rows[c], co + LANES:co + 2 * LANES] = b1.astype(BF16)
        emit_proj()
    assert not pieces


def _alibi_tables(lq, lk, q0, with_chunk_mask, scale=1.0):
    slopes = 2.0 ** (-8.0 * np.arange(1, ATTN_HEADS + 1) / ATTN_HEADS)
    dist = np.abs((q0 + np.arange(lq))[:, None] - np.arange(lk)[None, :]).astype(np.float64)
    base = -slopes.reshape(ATTN_KV_HEADS, ATTN_GROUP, 1, 1) * dist * scale
    base = base.reshape(ATTN_KV_HEADS, ATTN_GROUP * lq, lk)
    if not with_chunk_mask:
        return jnp.asarray(base[None], F32)
    tabs = []
    for tb in range(BAND):
        valid = (np.arange(lk) // CHUNK) >= (BAND - 1 - tb)
        tabs.append(np.where(valid[None, None, :], base, NEG))
    return jnp.asarray(np.stack(tabs), F32)


def _sink_rows(sinks, lq):
    return jnp.repeat(sinks.astype(F32).reshape(ATTN_KV_HEADS, ATTN_GROUP), lq, axis=1)[..., None]


def _resident(shape):
    nd = len(shape)
    return pl.BlockSpec(shape, lambda *_: (0,) * nd, pipeline_mode=pl.Buffered(1))


def _mixers_prompt(x, w_in, lbl, gnorm, sinks, layer, tt):
    B, T, D = x.shape
    nt = T // tt
    hist = (BAND - 1) * CHUNK
    w = min(WINDOW, T)
    bias = jnp.swapaxes(_alibi_tables(CHUNK, BAND * CHUNK, hist, True, LOG2E), -1, -2)
    sink = jnp.swapaxes(_sink_rows(sinks, CHUNK), -1, -2)
    kern = functools.partial(_mixer_prompt_kernel, layer, tt, nt)
    assert nt % 2 == 0
    n_tiles = B * nt
    half = nt // 2

    def tile(j):
        j = jnp.minimum(j, n_tiles - 1)
        return (j // nt, j % nt, 0)

    return pl.pallas_call(
        kern,
        grid=(n_tiles // 2,),
        in_specs=[
            pl.BlockSpec((1, tt, D), lambda k: (0, 0, 0), pipeline_mode=pl.Buffered(1)),
            pl.BlockSpec((1, tt, D), lambda k: tile(2 * k + 1)),
            pl.BlockSpec((1, tt, D), lambda k: tile(2 * k + 2)),
            _resident(w_in.shape),
            _resident(lbl.shape),
            _resident(gnorm.shape),
            _resident(bias.shape),
            _resident(sink.shape),
        ],
        out_specs=[
            pl.BlockSpec((1, 2 * tt, D), lambda k: (k // half, k % half, 0)),
            pl.BlockSpec((1, w, LANES), lambda k: (k // half, 0, 0)),
            pl.BlockSpec((1, w, LANES), lambda k: (k // half, 0, 0)),
            pl.BlockSpec((1, HG_HEADS, HG_DK, HG_DV), lambda k: (k // half, 0, 0, 0)),
        ],
        out_shape=[
            jax.ShapeDtypeStruct((B, T, D), BF16),
            jax.ShapeDtypeStruct((B, w, LANES), F32),
            jax.ShapeDtypeStruct((B, w, LANES), F32),
            jax.ShapeDtypeStruct((B, HG_HEADS, HG_DK, HG_DV), F32),
        ],
        scratch_shapes=[
            pltpu.VMEM((tt, IN_COLS), F32),
            pltpu.VMEM((tt, IN_COLS), F32),
            pltpu.VMEM((2, tt, D), BF16),
            pltpu.VMEM((hist, LANES), F32),
            pltpu.VMEM((hist, LANES), F32),
            pltpu.VMEM((ATTN_KV_HEADS, hist + tt, LANES), BF16),
            pltpu.VMEM((2, ATTN_KV_HEADS, LANES, hist + tt), BF16),
            pltpu.VMEM((HG_HEADS, HG_DV, HG_DK), F32),
        ],
        compiler_params=pltpu.CompilerParams(
            dimension_semantics=("arbitrary",), vmem_limit_bytes=VMEM_LIMIT),
        name="mixers_prompt",
    )(x, x, x, w_in, lbl, gnorm, bias, sink)


def _mixer_sample_kernel(layer, nb, ts, x_ref, w_in_ref, lbl_ref, gnorm_ref, bias_ref, sink_ref,
                         kc_ref, vc_ref, s0_ref, mix_ref, kn_ref, vn_ref, sn_ref, z_scr):
    wlen = kc_ref.shape[1]
    z_scr[...] = _dot(x_ref[...].astype(BF16), w_in_ref[...])
    kn_ref[...] = z_scr[:, C_AK:C_AK + LANES]
    vn_ref[...] = z_scr[:, C_AV:C_AV + LANES]

    lb = _lower_bound(lbl_ref[...], layer)
    gnorm = gnorm_ref[...]
    tril_mask, tril3 = _tril_consts(ts)
    low = lax.broadcasted_iota(jnp.int32, (ts, LANES), 1) < ATTN_HEAD_DIM
    low_k = lax.broadcasted_iota(jnp.int32, (wlen + ts, LANES), 1) < ATTN_HEAD_DIM

    def batch_body(b, carry):
        r0 = pl.multiple_of(b * ts, ts)
        rows = pl.ds(r0, ts)
        st = [s0_ref[b, h].T for h in range(HG_HEADS)]
        qg, kg, kd, v, eGL = _hgrn_gates(z_scr[rows, C_HQ:C_HQ + HG_WIDTH], z_scr[rows, C_HF:C_HF + HG_WIDTH],
                                         z_scr[rows, C_HI:C_HI + HG_WIDTH], lb, tril3)
        A, U = _hgrn_intra(qg, kg, kd, v, tril_mask)
        o_h = _hgrn_out(A, qg, v, [s.astype(BF16) for s in st], z_scr[rows, C_HG:C_HG + HG_WIDTH], gnorm)
        for h in range(HG_HEADS):
            sn_ref[b, h] = (st[h] * eGL[:, _head(h)] + U[h]).T
        mix_ref[rows, 0:HG_WIDTH] = o_h.astype(BF16)

        k_all = jnp.concatenate([kc_ref[b], z_scr[rows, C_AK:C_AK + LANES]], axis=0)
        v_all = jnp.concatenate([vc_ref[b], z_scr[rows, C_AV:C_AV + LANES]], axis=0)
        k2 = _dup_halves(k_all, low_k)
        v2 = _dup_halves(v_all, low_k)
        for g in range(ATTN_KV_HEADS):
            cq = C_AQ + g * ATTN_GROUP * ATTN_HEAD_DIM
            s = _swa_scores(z_scr[rows, cq:cq + LANES], z_scr[rows, cq + LANES:cq + 2 * LANES],
                            k2[g].astype(BF16), bias_ref[0, g], low)
            b0, b1 = _swa_out(*_swa_probs(s, sink_ref[g]), v2[g].astype(BF16), low)
            co = HG_WIDTH + g * ATTN_GROUP * ATTN_HEAD_DIM
            mix_ref[rows, co:co + LANES] = b0.astype(BF16)
            mix_ref[rows, co + LANES:co + 2 * LANES] = b1.astype(BF16)
        return carry

    lax.fori_loop(0, nb, batch_body, 0, unroll=2)


def _mixers_sample(x, w_in, lbl, gnorm, sinks, k_cache, v_cache, s0, layer):
    nb, ts, D = x.shape
    wlen = k_cache.shape[1]
    bias = _alibi_tables(ts, wlen + ts, wlen, False)
    sink = _sink_rows(sinks, ts)
    kern = functools.partial(_mixer_sample_kernel, layer, nb, ts)
    n = nb * ts
    return pl.pallas_call(
        kern,
        out_shape=[
            jax.ShapeDtypeStruct((n, D), BF16),
            jax.ShapeDtypeStruct((n, LANES), F32),
            jax.ShapeDtypeStruct((n, LANES), F32),
            jax.ShapeDtypeStruct((nb, HG_HEADS, HG_DK, HG_DV), F32),
        ],
        scratch_shapes=[pltpu.VMEM((n, IN_COLS), F32)],
        compiler_params=pltpu.CompilerParams(vmem_limit_bytes=VMEM_LIMIT),
        name="mixers_sample",
    )(x.reshape(n, D), w_in, lbl, gnorm, bias, sink,
      k_cache.reshape(nb, wlen, LANES), v_cache.reshape(nb, wlen, LANES), s0)


def _mem_kv_kernel(d, m_ref, w_ref, k_ref, v_ref, kb_ref, vb_ref):
    hd = d // MEM_HEADS
    r = _dot(m_ref[...].astype(BF16), w_ref[...])
    for h in range(MEM_HEADS):
        k_ref[:, h, :] = r[:, h * hd:(h + 1) * hd]
        v_ref[:, h, :] = r[:, d + h * hd:d + (h + 1) * hd]
    kb_ref[...] = r[:, :d].astype(BF16)
    vb_ref[...] = r[:, d:].astype(BF16)


def _mem_kv(mem, w_kv, tm):
    n, d = mem.shape
    hd = d // MEM_HEADS
    row = lambda i: (i, 0)
    row3 = lambda i: (i, 0, 0)
    return pl.pallas_call(
        functools.partial(_mem_kv_kernel, d),
        grid=(n // tm,),
        in_specs=[pl.BlockSpec((tm, d), row), _resident(w_kv.shape)],
        out_specs=[pl.BlockSpec((tm, MEM_HEADS, hd), row3)] * 2 + [pl.BlockSpec((tm, d), row)] * 2,
        out_shape=[jax.ShapeDtypeStruct((n, MEM_HEADS, hd), F32)] * 2 + [jax.ShapeDtypeStruct((n, d), BF16)] * 2,
        compiler_params=pltpu.CompilerParams(
            dimension_semantics=("arbitrary",), vmem_limit_bytes=VMEM_LIMIT),
        name="mem_kv",
    )(mem, w_kv)


def _post_kernel(alpha, nb, tb, n_sub, x_ref, mix_ref, mk_ref, mv_ref, w_out_ref, w_q_ref, w_o_ref,
                 w_fi_ref, w_fo_ref, g_ref, b_ref, y_ref, o_scr, act_scr):
    d = x_ref.shape[-1]
    n = nb * tb
    ns = n // n_sub
    dff = w_fo_ref.shape[0]
    hd = d // MEM_HEADS
    subs = range(n_sub)
    rows = [slice(s * ns, (s + 1) * ns) for s in subs]
    x_all = x_ref[...].reshape(n, d)
    mix_all = mix_ref[...].reshape(n, d)

    x1 = [_layer_norm(alpha * x_all[rows[s]] + _dot(mix_all[rows[s]], w_out_ref[...]),
                      g_ref[0:1, :], b_ref[0:1, :]) for s in subs]
    q = [_dot(x1[s].astype(BF16), w_q_ref[...]).astype(BF16) for s in subs]

    seg = tb if tb < ns else ns
    pieces = [(slice(r0, r0 + seg), r0 // tb) for r0 in range(0, n, seg)]
    for h in range(MEM_HEADS):
        cols = slice(h * hd, (h + 1) * hd)
        for r, i in pieces:
            qh = q[r.start // ns][r.start % ns:r.start % ns + seg, cols]
            s = _dot_nt(qh, mk_ref[i, :, cols].astype(BF16)) * (hd ** -0.5)
            m = jnp.max(s, axis=-1, keepdims=True)
            p = jnp.exp(s - m)
            den = jnp.sum(p, axis=-1, keepdims=True)
            o = _dot(p.astype(BF16), mv_ref[i, :, cols].astype(BF16)) / den
            o_scr[r, cols] = o.astype(BF16)
    x2 = [_layer_norm(alpha * x1[s] + _dot(o_scr[rows[s], :], w_o_ref[...]),
                      g_ref[1:2, :], b_ref[1:2, :]) for s in subs]

    x2b = [x2[s].astype(BF16) for s in subs]
    for j in range(dff // MXU_COLS):
        cg = slice(j * MXU_COLS, (j + 1) * MXU_COLS)
        cu = slice(dff + j * MXU_COLS, dff + (j + 1) * MXU_COLS)
        for s in subs:
            gte = _dot(x2b[s], w_fi_ref[:, cg])
            up = _dot(x2b[s], w_fi_ref[:, cu])
            act_scr[rows[s], cg] = (gte * _sigmoid(gte) * up).astype(BF16)
    x3 = [_layer_norm(alpha * x2[s] + _dot(act_scr[rows[s], :], w_fo_ref[...]),
                      g_ref[2:3, :], b_ref[2:3, :]) for s in subs]
    y_ref[...] = jnp.concatenate(x3, axis=0).reshape(nb, tb, d)


def _post_blocks(x, mix, mk, mv, w_out, w_q, w_o, w_fi, w_fo, ln_g, ln_b, alpha, nb, tb, n_sub, vmem_limit):
    B, T, d = x.shape
    dff = w_fo.shape[0]
    m_tok = mk.shape[1]
    n = nb * tb
    tile = lambda b, t: (b, t, 0)
    per_b = lambda b, t: (b, 0, 0)
    kern = functools.partial(_post_kernel, alpha, nb, tb, n_sub)
    return pl.pallas_call(
        kern,
        grid=(B // nb, T // tb),
        in_specs=[
            pl.BlockSpec((nb, tb, d), tile),
            pl.BlockSpec((nb, tb, d), tile),
            pl.BlockSpec((nb, m_tok, d), per_b),
            pl.BlockSpec((nb, m_tok, d), per_b),
            _resident(w_out.shape), _resident(w_q.shape), _resident(w_o.shape),
            _resident(w_fi.shape), _resident(w_fo.shape),
            _resident(ln_g.shape), _resident(ln_b.shape),
        ],
        out_specs=pl.BlockSpec((nb, tb, d), tile),
        out_shape=jax.ShapeDtypeStruct((B, T, d), F32),
        scratch_shapes=[pltpu.VMEM((n, d), BF16), pltpu.VMEM((n, dff), BF16)],
        compiler_params=pltpu.CompilerParams(
            dimension_semantics=("arbitrary", "arbitrary"), vmem_limit_bytes=vmem_limit),
        name="post_blocks",
    )(x, mix, mk, mv, w_out, w_q, w_o, w_fi, w_fo, ln_g, ln_b)


TT_PROMPT = 512
TM_POST = 1024
SUB_POST = 4
TM_MEM = 512


def kernel(x_prompt, x_sample, cache_swa_k, cache_swa_v, state_hgrn, cache_mem_k, cache_mem_v, mem_prompt, w_in, hgrn_lb_logits, hgrn_norm_g, attn_sinks, w_out, w_mem_q, w_mem_kv, w_mem_o, w_ffn_in, w_ffn_out, ln_g, ln_b):
    depth = w_in.shape[0]
    alpha = (2.0 * depth) ** 0.25
    B, T, D = x_prompt.shape
    nbs, ts, _ = x_sample.shape
    m_tok = mem_prompt.shape[1]
    lbl = hgrn_lb_logits.astype(F32)

    yp, ys = x_prompt, x_sample
    outs = [[] for _ in range(8)]
    for l in range(depth):
        w_in_l = w_in[l].astype(BF16)
        w_out_l = w_out[l].astype(BF16)
        w_q_l = w_mem_q[l].astype(BF16)
        w_kv_l = w_mem_kv[l].astype(BF16)
        w_o_l = w_mem_o[l].astype(BF16)
        w_fi_l = w_ffn_in[l].astype(BF16)
        w_fo_l = w_ffn_out[l].astype(BF16)
        gnorm = hgrn_norm_g[l].reshape(1, HG_DV).astype(F32)

        mix_p, k_p, v_p, s_p = _mixers_prompt(yp, w_in_l, lbl, gnorm, attn_sinks[l], l, TT_PROMPT)
        mk_p, mv_p, mk_b, mv_b = _mem_kv(mem_prompt.reshape(B * m_tok, D), w_kv_l, TM_MEM)
        yp = _post_blocks(yp, mix_p, mk_b.reshape(B, m_tok, D), mv_b.reshape(B, m_tok, D),
                          w_out_l, w_q_l, w_o_l, w_fi_l, w_fo_l, ln_g[l], ln_b[l], alpha, 1, TM_POST, SUB_POST,
                          VMEM_LIMIT_POST)

        mix_s, k_s, v_s, s_s = _mixers_sample(ys, w_in_l, lbl, gnorm, attn_sinks[l],
                                              cache_swa_k[l], cache_swa_v[l], state_hgrn[l], l)
        ys = _post_blocks(ys, mix_s.reshape(nbs, ts, D), cache_mem_k[l].astype(BF16).reshape(nbs, m_tok, D),
                          cache_mem_v[l].astype(BF16).reshape(nbs, m_tok, D),
                          w_out_l, w_q_l, w_o_l, w_fi_l, w_fo_l, ln_g[l], ln_b[l], alpha, nbs, ts, 1, VMEM_LIMIT)

        wlen = k_p.shape[1]
        new = [k_p.reshape(B, wlen, ATTN_KV_HEADS, ATTN_HEAD_DIM),
               v_p.reshape(B, wlen, ATTN_KV_HEADS, ATTN_HEAD_DIM),
               s_p,
               mk_p.reshape(B, m_tok, MEM_HEADS, D // MEM_HEADS),
               mv_p.reshape(B, m_tok, MEM_HEADS, D // MEM_HEADS),
               k_s.reshape(nbs, ts, ATTN_KV_HEADS, ATTN_HEAD_DIM),
               v_s.reshape(nbs, ts, ATTN_KV_HEADS, ATTN_HEAD_DIM),
               s_s]
        for acc, a in zip(outs, new):
            acc.append(a)
    return (yp, ys) + tuple(jnp.stack(a) for a in outs)
```

```python
import functools

import numpy as np
import jax
import jax.numpy as jnp
from jax import lax
from jax.experimental import pallas as pl
from jax.experimental.pallas import tpu as pltpu

F32 = jnp.float32
BF16 = jnp.bfloat16

CHUNK = 64
HG_HEADS = 4
HG_DK = 128
HG_DV = 128
HG_WIDTH = HG_HEADS * HG_DV
ATTN_HEADS = 8
ATTN_KV_HEADS = 2
ATTN_HEAD_DIM = 64
ATTN_GROUP = ATTN_HEADS // ATTN_KV_HEADS
ATTN_WIDTH = ATTN_HEADS * ATTN_HEAD_DIM
WINDOW = 128
BAND = WINDOW // CHUNK + 1
MEM_HEADS = 4
NEG = -1e30

C_HQ = 0
C_HF = C_HQ + HG_HEADS * HG_DK
C_HI = C_HF + HG_HEADS * HG_DK
C_HG = C_HI + HG_WIDTH
C_AQ = C_HG + HG_WIDTH
C_AK = C_AQ + ATTN_WIDTH
C_AV = C_AK + ATTN_KV_HEADS * ATTN_HEAD_DIM
IN_COLS = C_AV + ATTN_KV_HEADS * ATTN_HEAD_DIM

LANES = 128
MXU_COLS = 256
MIB = 1024 * 1024
VMEM_LIMIT_POST = 60 * MIB
VMEM_LIMIT = 48 * MIB


LOG2E = 1.4426950408889634


def _sigmoid(x):
    return 1.0 / (1.0 + jnp.exp2(x * (-LOG2E)))


def _dot(a, b):
    return jnp.dot(a, b, preferred_element_type=F32)


def _dot_nt(a, b):
    return lax.dot_general(a, b, (((1,), (1,)), ((), ())), preferred_element_type=F32)


def _dot_tn(a, b):
    return lax.dot_general(a, b, (((0,), (0,)), ((), ())), preferred_element_type=F32)


def _layer_norm(y, g, b):
    mu = jnp.mean(y, axis=-1, keepdims=True)
    var = jnp.mean(y * y, axis=-1, keepdims=True) - mu * mu
    return (y - mu) * lax.rsqrt(var + 1e-5) * g + b


def _lower_bound(lbl, layer):
    n = lbl.shape[0]
    rows = [lbl[i:i + 1, :] for i in range(n)]
    m = functools.reduce(jnp.maximum, rows)
    e = [jnp.exp(r - m) for r in rows]
    tot = functools.reduce(jnp.add, e)
    return functools.reduce(jnp.add, e[:layer + 1]) / tot


def _tril_consts(L):
    r = lax.broadcasted_iota(jnp.int32, (L, L), 0)
    c = lax.broadcasted_iota(jnp.int32, (L, L), 1)
    mask = c <= r
    tril = jnp.where(mask, 1.0, 0.0).astype(BF16)
    return mask, jnp.concatenate([tril, tril, tril], axis=1)


def _hgrn_gates(hq, hf, hi, lb, tril3):
    L = hq.shape[0]
    f = lb + (1.0 - lb) * _sigmoid(hf)
    logf = jnp.log(f)
    p0 = logf.astype(BF16)
    r0 = logf - p0.astype(F32)
    p1 = r0.astype(BF16)
    p2 = (r0 - p1.astype(F32)).astype(BF16)
    G = _dot(tril3, jnp.concatenate([p0, p1, p2], axis=0))
    eGL = jnp.exp(G[L - 1:L, :])
    qg = hq * _sigmoid(hq) * jnp.exp(G)
    kg = (1.0 - f) * jnp.exp2(G * (-LOG2E))
    return qg.astype(BF16), kg.astype(BF16), (kg * eGL).astype(BF16), hi.astype(BF16), eGL


def _head(h):
    return slice(h * HG_DK, (h + 1) * HG_DK)


def _hgrn_intra(qg, kg, kd, v, tril_mask):
    A = [jnp.where(tril_mask, _dot_nt(qg[:, _head(h)], kg[:, _head(h)]), 0.0).astype(BF16)
         for h in range(HG_HEADS)]
    U = [_dot_tn(v[:, _head(h)], kd[:, _head(h)]) for h in range(HG_HEADS)]
    return A, U


def _hgrn_out(A, qg, v, s_bf, hg, gnorm):
    gate = hg * _sigmoid(hg)
    outs = []
    for h in range(HG_HEADS):
        o = _dot(A[h], v[:, _head(h)]) + _dot(qg[:, _head(h)], s_bf[h])
        ms = jnp.mean(o * o, axis=-1, keepdims=True)
        outs.append(o * lax.rsqrt(ms + 1e-6) * gnorm * gate[:, _head(h)])
    return jnp.concatenate(outs, axis=1)


def _dup_halves(a, low):
    sw = pltpu.roll(a, ATTN_HEAD_DIM, axis=1)
    return jnp.where(low, a, sw), jnp.where(low, sw, a)


def _swa_scores(qb0, qb1, k2, bias, low):
    return _dot_nt(_stack_q(qb0, qb1, low), k2) * (ATTN_HEAD_DIM ** -0.5) + bias


def _swa_probs(s, sink):
    m = jnp.maximum(jnp.max(s, axis=-1, keepdims=True), sink)
    p = jnp.exp(s - m)
    return p.astype(BF16), jnp.sum(p, axis=-1, keepdims=True) + jnp.exp(sink - m)


def _swa_out(p, den, v2, low):
    Lq = p.shape[0] // ATTN_GROUP
    o = _dot(p, v2) / den
    return (jnp.where(low, o[0:Lq], o[Lq:2 * Lq]), jnp.where(low, o[2 * Lq:3 * Lq], o[3 * Lq:4 * Lq]))


def _stack_q(qb0, qb1, low):
    return jnp.concatenate([jnp.where(low, qb0, 0.0), jnp.where(low, 0.0, qb0),
                            jnp.where(low, qb1, 0.0), jnp.where(low, 0.0, qb1)], axis=0).astype(BF16)


def _swa_probs_t(k2, qm, bias_t, sink_t):
    s = _dot_nt(k2, qm) * (ATTN_HEAD_DIM ** -0.5 * LOG2E) + bias_t
    m = jnp.maximum(jnp.max(s, axis=0, keepdims=True), sink_t)
    e = jnp.exp2(s - m)
    den = jnp.sum(e, axis=0, keepdims=True) + jnp.exp2(sink_t - m)
    return (e * (1.0 / den)).astype(BF16)


def _swa_out_t(vt2, p_t, low):
    Lq = p_t.shape[1] // ATTN_GROUP
    o_t = _dot(vt2, p_t)
    a0 = o_t[:, 0:2 * Lq].T
    a1 = o_t[:, 2 * Lq:4 * Lq].T
    return jnp.where(low, a0[0:Lq], a0[Lq:2 * Lq]), jnp.where(low, a1[0:Lq], a1[Lq:2 * Lq])


def _mixer_prompt_kernel(layer, tt, nt, x0_ref, xa_ref, xb_ref, w_in_ref, lbl_ref, gnorm_ref, bias_ref,
                         sink_ref, mix_ref, kp_ref, vp_ref, sp_ref,
                         za_scr, zb_scr, xc_scr, kh_scr, vh_scr, k2_scr, vt_scr, st_scr):
    k = pl.program_id(0)
    t0 = lax.rem(2 * k, nt)

    @pl.when(k == 0)
    def _():
        za_scr[...] = _dot(x0_ref[0].astype(BF16), w_in_ref[...])

    @pl.when(t0 == 0)
    def _():
        st_scr[...] = jnp.zeros_like(st_scr)
        kh_scr[...] = jnp.zeros_like(kh_scr)
        vh_scr[...] = jnp.zeros_like(vh_scr)

    consts = (_lower_bound(lbl_ref[...], layer), gnorm_ref[...])
    scr = (kh_scr, vh_scr, k2_scr, vt_scr, st_scr)

    _mix_tile(tt, za_scr, t0, 0, consts, bias_ref, sink_ref, mix_ref, scr,
              _proj_pieces(xa_ref, xc_scr.at[0], w_in_ref, zb_scr))
    _mix_tile(tt, zb_scr, t0 + 1, tt, consts, bias_ref, sink_ref, mix_ref, scr,
              _proj_pieces(xb_ref, xc_scr.at[1], w_in_ref, za_scr))

    @pl.when(t0 + 1 == nt - 1)
    def _():
        w = min(WINDOW, tt)
        kp_ref[0] = zb_scr[tt - w:tt, C_AK:C_AK + LANES]
        vp_ref[0] = zb_scr[tt - w:tt, C_AV:C_AV + LANES]
        for h in range(HG_HEADS):
            sp_ref[0, h] = st_scr[h].T


def _proj_pieces(x_ref, xb_scr, w_in_ref, z_dst):
    half = x_ref.shape[1] // PROJ_ROW_SPLIT

    def piece(j, r):
        cols = slice(j * MXU_COLS, (j + 1) * MXU_COLS)
        rows = slice(r * half, (r + 1) * half)
        def run():
            if j == 0:
                xb_scr[rows, :] = x_ref[0, rows, :].astype(BF16)
            z_dst[rows, cols] = _dot(xb_scr[rows, :], w_in_ref[:, cols])
        return run
    return [piece(j, r) for j in range(IN_COLS // MXU_COLS) for r in range(PROJ_ROW_SPLIT)]


def _mix_tile(tt, z_scr, t, row0, consts, bias_ref, sink_ref, mix_ref, scr, next_proj):
    lb, gnorm = consts
    kh_scr, vh_scr, k2_scr, vt_scr, st_scr = scr
    n_chunks = tt // CHUNK
    hist = (BAND - 1) * CHUNK

    k_tile = z_scr[:, C_AK:C_AK + LANES]
    low_k = lax.broadcasted_iota(jnp.int32, (hist + tt, LANES), 1) < ATTN_HEAD_DIM
    ka, kb = _dup_halves(jnp.concatenate([kh_scr[...], k_tile], axis=0), low_k)
    k2_scr[0] = ka.astype(BF16)
    k2_scr[1] = kb.astype(BF16)
    v_tile = z_scr[:, C_AV:C_AV + LANES]
    v_hist = vh_scr[...]
    v_shift = (jnp.concatenate([v_hist, v_tile], axis=0),
               jnp.concatenate([v_hist[CHUNK:], v_tile, v_tile[tt - CHUNK:]], axis=0))
    for par in range(2):
        vt = v_shift[par].T.astype(BF16)
        for g in range(ATTN_KV_HEADS):
            blk = vt[g * ATTN_HEAD_DIM:(g + 1) * ATTN_HEAD_DIM]
            vt_scr[par, g] = jnp.concatenate([blk, blk], axis=0)
    kh_scr[...] = k_tile[tt - hist:]
    vh_scr[...] = v_tile[tt - hist:]

    tril_mask, tril3 = _tril_consts(CHUNK)
    low = lax.broadcasted_iota(jnp.int32, (CHUNK, LANES), 1) < ATTN_HEAD_DIM
    chunks = range(n_chunks)
    groups = range(ATTN_KV_HEADS)
    rows = [slice(c * CHUNK, (c + 1) * CHUNK) for c in chunks]
    out_rows = [slice(row0 + c * CHUNK, row0 + (c + 1) * CHUNK) for c in chunks]
    keys = [slice(c * CHUNK, (c + BAND) * CHUNK) for c in chunks]
    z = lambda c, col: z_scr[rows[c], col:col + HG_WIDTH]

    pieces = list(next_proj)
    n_pieces, n_slots, slot = len(pieces), 4 * n_chunks, [0]

    def emit_proj():
        slot[0] += 1
        while pieces and (n_pieces - len(pieces)) * n_slots < slot[0] * n_pieces:
            pieces.pop(0)()

    gates = []
    for c in chunks:
        gates.append(_hgrn_gates(z(c, C_HQ), z(c, C_HF), z(c, C_HI), lb, tril3))
        emit_proj()

    def probs(c, g):
        tb = BAND - 1 if c >= BAND - 1 else jnp.minimum(t * n_chunks + c, BAND - 1)
        cq = C_AQ + g * ATTN_GROUP * ATTN_HEAD_DIM
        qm = _stack_q(z_scr[rows[c], cq:cq + LANES], z_scr[rows[c], cq + LANES:cq + 2 * LANES], low)
        return _swa_probs_t(k2_scr[g, keys[c], :], qm, bias_ref[tb, g], sink_ref[g] * LOG2E)

    p_all = []
    for c in chunks:
        p_all.append([probs(c, g) for g in groups])
        emit_proj()
    intra = [_hgrn_intra(*gates[c][:4], tril_mask) for c in chunks]

    st = [st_scr[h] for h in range(HG_HEADS)]
    for c in chunks:
        qg, _, _, v, eGL = gates[c]
        A, U = intra[c]
        o_h = _hgrn_out(A, qg, v, [s.T.astype(BF16) for s in st], z(c, C_HG), gnorm)
        st = [st[h] * eGL[:, _head(h)] + U[h] for h in range(HG_HEADS)]
        mix_ref[0, out_rows[c], 0:HG_WIDTH] = o_h.astype(BF16)
        emit_proj()
    for h in range(HG_HEADS):
        st_scr[h] = st[h]

    for c in chunks:
        par = c % 2
        win = slice((c - par) * CHUNK, (c - par + BAND) * CHUNK)
        for g in groups:
            b0, b1 = _swa_out_t(vt_scr[par, g, :, win], p_all[c][g], low)
            co = HG_WIDTH + g * ATTN_GROUP * ATTN_HEAD_DIM
            mix_ref[0, out_rows[c], co:co + LANES] = b0.astype(BF16)
            mix_ref[0, out_rows[c], co + LANES:co + 2 * LANES] = b1.astype(BF16)
        emit_proj()
    assert not pieces


def _alibi_tables(lq, lk, q0, with_chunk_mask, scale=1.0):
    slopes = 2.0 ** (-8.0 * np.arange(1, ATTN_HEADS + 1) / ATTN_HEADS)
    dist = np.abs((q0 + np.arange(lq))[:, None] - np.arange(lk)[None, :]).astype(np.float64)
    base = -slopes.reshape(ATTN_KV_HEADS, ATTN_GROUP, 1, 1) * dist * scale
    base = base.reshape(ATTN_KV_HEADS, ATTN_GROUP * lq, lk)
    if not with_chunk_mask:
        return jnp.asarray(base[None], F32)
    tabs = []
    for tb in range(BAND):
        valid = (np.arange(lk) // CHUNK) >= (BAND - 1 - tb)
        tabs.append(np.where(valid[None, None, :], base, NEG))
    return jnp.asarray(np.stack(tabs), F32)


def _sink_rows(sinks, lq):
    return jnp.repeat(sinks.astype(F32).reshape(ATTN_KV_HEADS, ATTN_GROUP), lq, axis=1)[..., None]


def _resident(shape):
    nd = len(shape)
    return pl.BlockSpec(shape, lambda *_: (0,) * nd, pipeline_mode=pl.Buffered(1))


def _mixers_prompt(x, w_in, lbl, gnorm, sinks, layer, tt):
    B, T, D = x.shape
    nt = T // tt
    hist = (BAND - 1) * CHUNK
    w = min(WINDOW, T)
    bias = jnp.swapaxes(_alibi_tables(CHUNK, BAND * CHUNK, hist, True, LOG2E), -1, -2)
    sink = jnp.swapaxes(_sink_rows(sinks, CHUNK), -1, -2)
    kern = functools.partial(_mixer_prompt_kernel, layer, tt, nt)
    assert nt % 2 == 0
    n_tiles = B * nt
    half = nt // 2

    def tile(j):
        j = jnp.minimum(j, n_tiles - 1)
        return (j // nt, j % nt, 0)

    return pl.pallas_call(
        kern,
        grid=(n_tiles // 2,),
        in_specs=[
            pl.BlockSpec((1, tt, D), lambda k: (0, 0, 0), pipeline_mode=pl.Buffered(1)),
            pl.BlockSpec((1, tt, D), lambda k: tile(2 * k + 1)),
            pl.BlockSpec((1, tt, D), lambda k: tile(2 * k + 2)),
            _resident(w_in.shape),
            _resident(lbl.shape),
            _resident(gnorm.shape),
            _resident(bias.shape),
            _resident(sink.shape),
        ],
        out_specs=[
            pl.BlockSpec((1, 2 * tt, D), lambda k: (k // half, k % half, 0)),
            pl.BlockSpec((1, w, LANES), lambda k: (k // half, 0, 0)),
            pl.BlockSpec((1, w, LANES), lambda k: (k // half, 0, 0)),
            pl.BlockSpec((1, HG_HEADS, HG_DK, HG_DV), lambda k: (k // half, 0, 0, 0)),
        ],
        out_shape=[
            jax.ShapeDtypeStruct((B, T, D), BF16),
            jax.ShapeDtypeStruct((B, w, LANES), F32),
            jax.ShapeDtypeStruct((B, w, LANES), F32),
            jax.ShapeDtypeStruct((B, HG_HEADS, HG_DK, HG_DV), F32),
        ],
        scratch_shapes=[
            pltpu.VMEM((tt, IN_COLS), F32),
            pltpu.VMEM((tt, IN_COLS), F32),
            pltpu.VMEM((2, tt, D), BF16),
            pltpu.VMEM((hist, LANES), F32),
            pltpu.VMEM((hist, LANES), F32),
            pltpu.VMEM((ATTN_KV_HEADS, hist + tt, LANES), BF16),
            pltpu.VMEM((2, ATTN_KV_HEADS, LANES, hist + tt), BF16),
            pltpu.VMEM((HG_HEADS, HG_DV, HG_DK), F32),
        ],
        compiler_params=pltpu.CompilerParams(
            dimension_semantics=("arbitrary",), vmem_limit_bytes=VMEM_LIMIT),
        name="mixers_prompt",
    )(x, x, x, w_in, lbl, gnorm, bias, sink)


def _mixer_sample_kernel(layer, nb, ts, x_ref, w_in_ref, lbl_ref, gnorm_ref, bias_ref, sink_ref,
                         kc_ref, vc_ref, s0_ref, mix_ref, kn_ref, vn_ref, sn_ref, z_scr):
    wlen = kc_ref.shape[1]
    z_scr[...] = _dot(x_ref[...].astype(BF16), w_in_ref[...])
    kn_ref[...] = z_scr[:, C_AK:C_AK + LANES]
    vn_ref[...] = z_scr[:, C_AV:C_AV + LANES]

    lb = _lower_bound(lbl_ref[...], layer)
    gnorm = gnorm_ref[...]
    tril_mask, tril3 = _tril_consts(ts)
    low = lax.broadcasted_iota(jnp.int32, (ts, LANES), 1) < ATTN_HEAD_DIM
    low_k = lax.broadcasted_iota(jnp.int32, (wlen + ts, LANES), 1) < ATTN_HEAD_DIM

    def batch_body(b, carry):
        r0 = pl.multiple_of(b * ts, ts)
        rows = pl.ds(r0, ts)
        st = [s0_ref[b, h].T for h in range(HG_HEADS)]
        qg, kg, kd, v, eGL = _hgrn_gates(z_scr[rows, C_HQ:C_HQ + HG_WIDTH], z_scr[rows, C_HF:C_HF + HG_WIDTH],
                                         z_scr[rows, C_HI:C_HI + HG_WIDTH], lb, tril3)
        A, U = _hgrn_intra(qg, kg, kd, v, tril_mask)
        o_h = _hgrn_out(A, qg, v, [s0_ref[b, h].astype(BF16) for h in range(HG_HEADS)],
                        z_scr[rows, C_HG:C_HG + HG_WIDTH], gnorm)
        for h in range(HG_HEADS):
            sn_ref[b, h] = (st[h] * eGL[:, _head(h)] + U[h]).T
        mix_ref[rows, 0:HG_WIDTH] = o_h.astype(BF16)

        k_all = jnp.concatenate([kc_ref[b], z_scr[rows, C_AK:C_AK + LANES]], axis=0)
        v_all = jnp.concatenate([vc_ref[b], z_scr[rows, C_AV:C_AV + LANES]], axis=0)
        k2 = _dup_halves(k_all, low_k)
        v2 = _dup_halves(v_all, low_k)
        for g in range(ATTN_KV_HEADS):
            cq = C_AQ + g * ATTN_GROUP * ATTN_HEAD_DIM
            s = _swa_scores(z_scr[rows, cq:cq + LANES], z_scr[rows, cq + LANES:cq + 2 * LANES],
                            k2[g].astype(BF16), bias_ref[0, g], low)
            b0, b1 = _swa_out(*_swa_probs(s, sink_ref[g]), v2[g].astype(BF16), low)
            co = HG_WIDTH + g * ATTN_GROUP * ATTN_HEAD_DIM
            mix_ref[rows, co:co + LANES] = b0.astype(BF16)
            mix_ref[rows, co + LANES:co + 2 * LANES] = b1.astype(BF16)
        return carry

    lax.fori_loop(0, nb, batch_body, 0, unroll=2)


def _mixers_sample(x, w_in, lbl, gnorm, sinks, k_cache, v_cache, s0, layer):
    nb, ts, D = x.shape
    wlen = k_cache.shape[1]
    bias = _alibi_tables(ts, wlen + ts, wlen, False)
    sink = _sink_rows(sinks, ts)
    kern = functools.partial(_mixer_sample_kernel, layer, nb, ts)
    n = nb * ts
    return pl.pallas_call(
        kern,
        out_shape=[
            jax.ShapeDtypeStruct((n, D), BF16),
            jax.ShapeDtypeStruct((n, LANES), F32),
            jax.ShapeDtypeStruct((n, LANES), F32),
            jax.ShapeDtypeStruct((nb, HG_HEADS, HG_DK, HG_DV), F32),
        ],
        scratch_shapes=[pltpu.VMEM((n, IN_COLS), F32)],
        compiler_params=pltpu.CompilerParams(vmem_limit_bytes=VMEM_LIMIT),
        name="mixers_sample",
    )(x.reshape(n, D), w_in, lbl, gnorm, bias, sink,
      k_cache.reshape(nb, wlen, LANES), v_cache.reshape(nb, wlen, LANES), s0)


def _mem_kv_kernel(d, m_ref, w_ref, k_ref, v_ref, kb_ref, vb_ref):
    hd = d // MEM_HEADS
    r = _dot(m_ref[...].astype(BF16), w_ref[...])
    for h in range(MEM_HEADS):
        k_ref[:, h, :] = r[:, h * hd:(h + 1) * hd]
        v_ref[:, h, :] = r[:, d + h * hd:d + (h + 1) * hd]
    kb_ref[...] = r[:, :d].astype(BF16)
    vb_ref[...] = r[:, d:].astype(BF16)


def _mem_kv(mem, w_kv, tm):
    n, d = mem.shape
    hd = d // MEM_HEADS
    row = lambda i: (i, 0)
    row3 = lambda i: (i, 0, 0)
    return pl.pallas_call(
        functools.partial(_mem_kv_kernel, d),
        grid=(n // tm,),
        in_specs=[pl.BlockSpec((tm, d), row), _resident(w_kv.shape)],
        out_specs=[pl.BlockSpec((tm, MEM_HEADS, hd), row3)] * 2 + [pl.BlockSpec((tm, d), row)] * 2,
        out_shape=[jax.ShapeDtypeStruct((n, MEM_HEADS, hd), F32)] * 2 + [jax.ShapeDtypeStruct((n, d), BF16)] * 2,
        compiler_params=pltpu.CompilerParams(
            dimension_semantics=("arbitrary",), vmem_limit_bytes=VMEM_LIMIT),
        name="mem_kv",
    )(mem, w_kv)


def _post_kernel(alpha, nb, tb, n_sub, x_ref, mix_ref, mk_ref, mv_ref, w_out_ref, w_q_ref, w_o_ref,
                 w_fi_ref, w_fo_ref, g_ref, b_ref, y_ref, o_scr, act_scr):
    d = x_ref.shape[-1]
    n = nb * tb
    ns = n // n_sub
    dff = w_fo_ref.shape[0]
    hd = d // MEM_HEADS
    subs = range(n_sub)
    rows = [slice(s * ns, (s + 1) * ns) for s in subs]
    x_all = x_ref[...].reshape(n, d)
    mix_all = mix_ref[...].reshape(n, d)

    x1 = [_layer_norm(alpha * x_all[rows[s]] + _dot(mix_all[rows[s]], w_out_ref[...]),
                      g_ref[0:1, :], b_ref[0:1, :]) for s in subs]
    q = [_dot(x1[s].astype(BF16), w_q_ref[...]).astype(BF16) for s in subs]

    seg = tb if tb < ns else ns
    pieces = [(slice(r0, r0 + seg), r0 // tb) for r0 in range(0, n, seg)]
    for h in range(MEM_HEADS):
        cols = slice(h * hd, (h + 1) * hd)
        for r, i in pieces:
            qh = q[r.start // ns][r.start % ns:r.start % ns + seg, cols]
            s = _dot_nt(qh, mk_ref[i, :, cols].astype(BF16)) * (hd ** -0.5)
            m = jnp.max(s, axis=-1, keepdims=True)
            p = jnp.exp(s - m)
            den = jnp.sum(p, axis=-1, keepdims=True)
            o = _dot(p.astype(BF16), mv_ref[i, :, cols].astype(BF16)) / den
            o_scr[r, cols] = o.astype(BF16)
    x2 = [_layer_norm(alpha * x1[s] + _dot(o_scr[rows[s], :], w_o_ref[...]),
                      g_ref[1:2, :], b_ref[1:2, :]) for s in subs]

    x2b = [x2[s].astype(BF16) for s in subs]
    for j in range(dff // MXU_COLS):
        cg = slice(j * MXU_COLS, (j + 1) * MXU_COLS)
        cu = slice(dff + j * MXU_COLS, dff + (j + 1) * MXU_COLS)
        for s in subs:
            gte = _dot(x2b[s], w_fi_ref[:, cg])
            up = _dot(x2b[s], w_fi_ref[:, cu])
            act_scr[rows[s], cg] = (gte * _sigmoid(gte) * up).astype(BF16)
    x3 = [_layer_norm(alpha * x2[s] + _dot(act_scr[rows[s], :], w_fo_ref[...]),
                      g_ref[2:3, :], b_ref[2:3, :]) for s in subs]
    y_ref[...] = jnp.concatenate(x3, axis=0).reshape(nb, tb, d)


def _post_blocks(x, mix, mk, mv, w_out, w_q, w_o, w_fi, w_fo, ln_g, ln_b, alpha, nb, tb, n_sub, vmem_limit):
    B, T, d = x.shape
    dff = w_fo.shape[0]
    m_tok = mk.shape[1]
    n = nb * tb
    tile = lambda b, t: (b, t, 0)
    per_b = lambda b, t: (b, 0, 0)
    kern = functools.partial(_post_kernel, alpha, nb, tb, n_sub)
    return pl.pallas_call(
        kern,
        grid=(B // nb, T // tb),
        in_specs=[
            pl.BlockSpec((nb, tb, d), tile),
            pl.BlockSpec((nb, tb, d), tile),
            pl.BlockSpec((nb, m_tok, d), per_b),
            pl.BlockSpec((nb, m_tok, d), per_b),
            _resident(w_out.shape), _resident(w_q.shape), _resident(w_o.shape),
            _resident(w_fi.shape), _resident(w_fo.shape),
            _resident(ln_g.shape), _resident(ln_b.shape),
        ],
        out_specs=pl.BlockSpec((nb, tb, d), tile),
        out_shape=jax.ShapeDtypeStruct((B, T, d), F32),
        scratch_shapes=[pltpu.VMEM((n, d), BF16), pltpu.VMEM((n, dff), BF16)],
        compiler_params=pltpu.CompilerParams(
            dimension_semantics=("arbitrary", "arbitrary"), vmem_limit_bytes=vmem_limit),
        name="post_blocks",
    )(x, mix, mk, mv, w_out, w_q, w_o, w_fi, w_fo, ln_g, ln_b)


TT_PROMPT = 512
PROJ_ROW_SPLIT = 2
TM_POST = 1024
SUB_POST = 4
TM_MEM = 512


def kernel(x_prompt, x_sample, cache_swa_k, cache_swa_v, state_hgrn, cache_mem_k, cache_mem_v, mem_prompt, w_in, hgrn_lb_logits, hgrn_norm_g, attn_sinks, w_out, w_mem_q, w_mem_kv, w_mem_o, w_ffn_in, w_ffn_out, ln_g, ln_b):
    depth = w_in.shape[0]
    alpha = (2.0 * depth) ** 0.25
    B, T, D = x_prompt.shape
    nbs, ts, _ = x_sample.shape
    m_tok = mem_prompt.shape[1]
    lbl = hgrn_lb_logits.astype(F32)

    yp, ys = x_prompt, x_sample
    outs = [[] for _ in range(8)]
    for l in range(depth):
        w_in_l = w_in[l].astype(BF16)
        w_out_l = w_out[l].astype(BF16)
        w_q_l = w_mem_q[l].astype(BF16)
        w_kv_l = w_mem_kv[l].astype(BF16)
        w_o_l = w_mem_o[l].astype(BF16)
        w_fi_l = w_ffn_in[l].astype(BF16)
        w_fo_l = w_ffn_out[l].astype(BF16)
        gnorm = hgrn_norm_g[l].reshape(1, HG_DV).astype(F32)

        mix_p, k_p, v_p, s_p = _mixers_prompt(yp, w_in_l, lbl, gnorm, attn_sinks[l], l, TT_PROMPT)
        mk_p, mv_p, mk_b, mv_b = _mem_kv(mem_prompt.reshape(B * m_tok, D), w_kv_l, TM_MEM)
        yp = _post_blocks(yp, mix_p, mk_b.reshape(B, m_tok, D), mv_b.reshape(B, m_tok, D),
                          w_out_l, w_q_l, w_o_l, w_fi_l, w_fo_l, ln_g[l], ln_b[l], alpha, 1, TM_POST, SUB_POST,
                          VMEM_LIMIT_POST)

        mix_s, k_s, v_s, s_s = _mixers_sample(ys, w_in_l, lbl, gnorm, attn_sinks[l],
                                              cache_swa_k[l], cache_swa_v[l], state_hgrn[l], l)
        ys = _post_blocks(ys, mix_s.reshape(nbs, ts, D), cache_mem_k[l].astype(BF16).reshape(nbs, m_tok, D),
                          cache_mem_v[l].astype(BF16).reshape(nbs, m_tok, D),
                          w_out_l, w_q_l, w_o_l, w_fi_l, w_fo_l, ln_g[l], ln_b[l], alpha, nbs, ts, 1, VMEM_LIMIT)

        wlen = k_p.shape[1]
        new = [k_p.reshape(B, wlen, ATTN_KV_HEADS, ATTN_HEAD_DIM),
               v_p.reshape(B, wlen, ATTN_KV_HEADS, ATTN_HEAD_DIM),
               s_p,
               mk_p.reshape(B, m_tok, MEM_HEADS, D // MEM_HEADS),
               mv_p.reshape(B, m_tok, MEM_HEADS, D // MEM_HEADS),
               k_s.reshape(nbs, ts, ATTN_KV_HEADS, ATTN_HEAD_DIM),
               v_s.reshape(nbs, ts, ATTN_KV_HEADS, ATTN_HEAD_DIM),
               s_s]
        for acc, a in zip(outs, new):
            acc.append(a)
    return (yp, ys) + tuple(jnp.stack(a) for a in outs)
```
